```python
import math
import jax, jax.numpy as jnp
from jax import lax
import numpy as np


D_MODEL = 1024
BATCH = 2
SEQ = 8192
DEPTH = 1

MEM_LEN = 256
RMS_EPS = 1e-6
RW_HEADS = 8
RW_HEAD_DIM = 64
RW_WIDTH = RW_HEADS * RW_HEAD_DIM
DECAY_LORA = 64
AAA_LORA = 64
LNX_EPS = 64e-5
L2_EPS = 1e-12
SW_HEADS = 8
SW_KV_HEADS = 2
SW_GROUP = SW_HEADS // SW_KV_HEADS
SW_HEAD_DIM = 64
SW_WIDTH = SW_HEADS * SW_HEAD_DIM
SW_KV_WIDTH = SW_KV_HEADS * SW_HEAD_DIM
WINDOW = 128
BLOCK = 128
ROPE_THETA = 10000.0
NEG_INF = -1e30
X_HEADS = 4
X_HEAD_DIM = 128
X_WIDTH = X_HEADS * X_HEAD_DIM
N_BRANCHES = 3

IN_SPLITS = (RW_WIDTH, RW_WIDTH, RW_WIDTH, DECAY_LORA, AAA_LORA, RW_WIDTH,
             SW_WIDTH, SW_KV_WIDTH, SW_KV_WIDTH, SW_WIDTH,
             X_WIDTH, X_WIDTH,
             N_BRANCHES * D_MODEL)
IN_WIDTH = sum(IN_SPLITS)

kernel_name = 'hybrid_rwkv7_swa_sink_memxattn_gated'


def rms_norm(x, g):
    xf = x.astype(jnp.float32)
    y = xf * lax.rsqrt(jnp.mean(xf * xf, axis=-1, keepdims=True) + RMS_EPS)
    return (y * g.astype(jnp.float32)).astype(x.dtype)


def token_shift_mix(u, mu):
    prev = jnp.pad(u, ((0, 0), (1, 0), (0, 0)))[:, :-1]
    return u + (prev - u) * mu


def rope(x, pos):
    d = x.shape[-1]
    half = d // 2
    inv = ROPE_THETA ** (-jnp.arange(half, dtype=jnp.float32) / half)
    ang = pos.astype(jnp.float32)[..., None] * inv
    cos = jnp.cos(ang)[:, :, None, :]
    sin = jnp.sin(ang)[:, :, None, :]
    xf = x.astype(jnp.float32)
    x1, x2 = xf[..., :half], xf[..., half:]
    return jnp.concatenate([x1 * cos - x2 * sin, x2 * cos + x1 * sin], axis=-1).astype(x.dtype)


def rwkv7_scan(r, w, k, v, a, b):
    def step(S, inp):
        r_t, w_t, k_t, v_t, a_t, b_t = inp
        sa = jnp.einsum('bhvk,bhk->bhv', S, a_t)
        S = S * w_t[:, :, None, :] + sa[..., None] * b_t[:, :, None, :] + v_t[..., None] * k_t[:, :, None, :]
        y = jnp.einsum('bhvk,bhk->bhv', S, r_t)
        return S, y
    bsz, _, h, n = r.shape
    xs = tuple(jnp.moveaxis(t, 1, 0) for t in (r, w, k, v, a, b))
    s0 = jnp.zeros((bsz, h, n, n), jnp.float32)
    _, y = lax.scan(step, s0, xs)
    return jnp.moveaxis(y, 0, 1)


def rwkv7_mix(p_r, p_k, p_v, p_w, p_a, mu_rkv, mu_wa, w0, w2, a0, a2, k_k, k_a, r_k, lnx_g, lnx_b):
    f32 = jnp.float32
    bsz, t, _ = p_r.shape
    r = token_shift_mix(p_r, mu_rkv[0]).astype(f32)
    k = token_shift_mix(p_k, mu_rkv[1]).astype(f32)
    v = token_shift_mix(p_v, mu_rkv[2]).astype(f32)
    lw = token_shift_mix(p_w, mu_wa[0]).astype(f32)
    la = token_shift_mix(p_a, mu_wa[1]).astype(f32)
    w_log = -jax.nn.softplus(-(w0.astype(f32) + jnp.tanh(lw) @ w2.astype(f32))) - 0.5
    decay = jnp.exp(-jnp.exp(w_log))
    a = jax.nn.sigmoid(a0.astype(f32) + la @ a2.astype(f32))
    heads = lambda u: u.reshape(bsz, t, RW_HEADS, RW_HEAD_DIM)
    kk = heads(k * k_k.astype(f32))
    kk = kk / jnp.maximum(jnp.linalg.norm(kk, axis=-1, keepdims=True), L2_EPS)
    k = k * (1.0 + (a - 1.0) * k_a.astype(f32))
    rh, kh, vh, ah = heads(r), heads(k), heads(v), heads(a)
    y = rwkv7_scan(rh, heads(decay), kh, vh, -kk, kk * ah)
    mean = jnp.mean(y, axis=-1, keepdims=True)
    var = jnp.mean(jnp.square(y - mean), axis=-1, keepdims=True)
    y = ((y - mean) * lax.rsqrt(var + LNX_EPS)).reshape(bsz, t, RW_WIDTH) * lnx_g.astype(f32) + lnx_b.astype(f32)
    bonus = jnp.sum(rh * kh * r_k.astype(f32), axis=-1, keepdims=True) * vh
    return y + bonus.reshape(bsz, t, RW_WIDTH)


def sliding_window_sink_attn(q, k, v, pos, q_g, k_g, sinks):
    bsz, t, _ = q.shape
    nb = t // BLOCK
    q = rope(rms_norm(q.reshape(bsz, t, SW_HEADS, SW_HEAD_DIM), q_g), pos)
    k = rope(rms_norm(k.reshape(bsz, t, SW_KV_HEADS, SW_HEAD_DIM), k_g), pos)
    v = v.reshape(bsz, t, SW_KV_HEADS, SW_HEAD_DIM)
    qb = q.reshape(bsz, nb, BLOCK, SW_KV_HEADS, SW_GROUP, SW_HEAD_DIM)
    def band(u):
        ub = u.reshape(bsz, nb, BLOCK, SW_KV_HEADS, SW_HEAD_DIM)
        prev = jnp.pad(ub, ((0, 0), (1, 0), (0, 0), (0, 0), (0, 0)))[:, :-1]
        return jnp.concatenate([prev, ub], axis=2)
    kband, vband = band(k), band(v)
    scale = SW_HEAD_DIM ** -0.5
    s = jnp.einsum('bnqhgd,bnkhd->bnhgqk', qb, kband).astype(jnp.float32) * scale
    qi = jnp.arange(BLOCK)[:, None]
    kj = jnp.arange(2 * BLOCK)[None, :]
    rel = qi - kj + BLOCK
    allowed = (rel >= 0) & (rel < WINDOW)
    not_pad = (jnp.arange(nb)[:, None, None] > 0) | (kj >= BLOCK)[None]
    mask = allowed[None] & not_pad
    s = jnp.where(mask[None, :, None, None], s, NEG_INF)
    sink = jnp.broadcast_to(sinks.astype(jnp.float32).reshape(SW_KV_HEADS, SW_GROUP)[None, None, :, :, None, None],
                            s.shape[:-1] + (1,))
    p = jax.nn.softmax(jnp.concatenate([s, sink], axis=-1), axis=-1)[..., :-1]
    o = jnp.einsum('bnhgqk,bnkhd->bnqhgd', p.astype(v.dtype), vband)
    return o.reshape(bsz, t, SW_WIDTH)


def memory_cross_attn(q, mem, mem_norm_g, w_mem_kv, xq_g, xk_g):
    bsz, t, _ = q.shape
    m = mem.shape[1]
    q = rms_norm(q.reshape(bsz, t, X_HEADS, X_HEAD_DIM), xq_g)
    kv = rms_norm(mem, mem_norm_g) @ w_mem_kv
    km, vm = kv[..., :X_WIDTH], kv[..., X_WIDTH:]
    km = rms_norm(km.reshape(bsz, m, X_HEADS, X_HEAD_DIM), xk_g)
    vm = vm.reshape(bsz, m, X_HEADS, X_HEAD_DIM)
    s = jnp.einsum('bthd,bmhd->bhtm', q, km).astype(jnp.float32) * (X_HEAD_DIM ** -0.5)
    p = jax.nn.softmax(s, axis=-1)
    o = jnp.einsum('bhtm,bmhd->bthd', p.astype(vm.dtype), vm)
    return o.reshape(bsz, t, X_WIDTH)


def setup_inputs(seed: int = 0) -> dict:
    key = jax.random.key(seed)
    ks = iter(jax.random.split(key, 40))
    f32 = jnp.float32
    def nrm(shape, scale):
        return jax.random.normal(next(ks), shape, f32) * scale
    L = DEPTH
    x = jax.random.normal(next(ks), (BATCH, SEQ, D_MODEL), f32)
    mem = jax.random.normal(next(ks), (BATCH, MEM_LEN, D_MODEL), f32)
    offset = jax.random.randint(next(ks), (BATCH, 1), 0, 4096, dtype=jnp.int32)
    positions = jnp.arange(SEQ, dtype=jnp.int32)[None, :] + offset
    return {
        'x': x,
        'mem': mem,
        'positions': positions,
        'norm_g': 1.0 + nrm((L, D_MODEL), 0.05),
        'mem_norm_g': 1.0 + nrm((L, D_MODEL), 0.05),
        'w_in': nrm((L, D_MODEL, IN_WIDTH), D_MODEL ** -0.5),
        'mu_rkv': jax.random.uniform(next(ks), (L, 3, RW_WIDTH), f32),
        'mu_wa': jax.random.uniform(next(ks), (L, 2, DECAY_LORA), f32),
        'w0': jnp.linspace(-6.0, -1.0, RW_WIDTH, dtype=f32)[None, :] + nrm((L, RW_WIDTH), 0.1),
        'w2': nrm((L, DECAY_LORA, RW_WIDTH), 0.5 * DECAY_LORA ** -0.5),
        'a0': nrm((L, RW_WIDTH), 0.1),
        'a2': nrm((L, AAA_LORA, RW_WIDTH), 0.5 * AAA_LORA ** -0.5),
        'k_k': 0.85 + nrm((L, RW_WIDTH), 0.05),
        'k_a': 1.0 + nrm((L, RW_WIDTH), 0.05),
        'r_k': nrm((L, RW_HEADS, RW_HEAD_DIM), 0.1),
        'lnx_g': 1.0 + nrm((L, RW_WIDTH), 0.05),
        'lnx_b': nrm((L, RW_WIDTH), 0.01),
        'q_norm_g': 1.0 + nrm((L, SW_HEAD_DIM), 0.05),
        'k_norm_g': 1.0 + nrm((L, SW_HEAD_DIM), 0.05),
        'sinks': nrm((L, SW_HEADS), 1.0),
        'xq_norm_g': 1.0 + nrm((L, X_HEAD_DIM), 0.05),
        'xk_norm_g': 1.0 + nrm((L, X_HEAD_DIM), 0.05),
        'w_mem_kv': nrm((L, D_MODEL, 2 * X_WIDTH), D_MODEL ** -0.5),
        'w_proj_a': nrm((L, RW_WIDTH, D_MODEL), RW_WIDTH ** -0.5),
        'w_proj_b': nrm((L, SW_WIDTH, D_MODEL), SW_WIDTH ** -0.5),
        'w_proj_c': nrm((L, X_WIDTH, D_MODEL), X_WIDTH ** -0.5),
        'w_out': nrm((L, D_MODEL, D_MODEL), D_MODEL ** -0.5),
    }


def reference(x, mem, positions, norm_g, mem_norm_g, w_in, mu_rkv, mu_wa, w0, w2, a0, a2, k_k, k_a, r_k,
              lnx_g, lnx_b, q_norm_g, k_norm_g, sinks, xq_norm_g, xk_norm_g, w_mem_kv,
              w_proj_a, w_proj_b, w_proj_c, w_out):
    bsz, t, _ = x.shape
    split_points = [int(i) for i in np.cumsum(IN_SPLITS)[:-1]]
    for l in range(DEPTH):
        h = rms_norm(x, norm_g[l])
        p = h @ w_in[l]
        (p_r, p_k, p_v, p_w, p_a, z_a,
         p_q, p_sk, p_sv, z_b,
         p_xq, z_c, p_gate) = jnp.split(p, split_points, axis=-1)
        y_a = rwkv7_mix(p_r, p_k, p_v, p_w, p_a, mu_rkv[l], mu_wa[l], w0[l], w2[l], a0[l], a2[l],
                        k_k[l], k_a[l], r_k[l], lnx_g[l], lnx_b[l]).astype(x.dtype) * jax.nn.silu(z_a)
        y_b = sliding_window_sink_attn(p_q, p_sk, p_sv, positions, q_norm_g[l], k_norm_g[l],
                                       sinks[l]) * jax.nn.silu(z_b)
        y_c = memory_cross_attn(p_xq, mem, mem_norm_g[l], w_mem_kv[l], xq_norm_g[l],
                                xk_norm_g[l]) * jax.nn.silu(z_c)
        gates = jax.nn.sigmoid(p_gate.astype(jnp.float32)).reshape(bsz, t, N_BRANCHES, D_MODEL)
        merged = (gates[:, :, 0] * (y_a @ w_proj_a[l]) + gates[:, :, 1] * (y_b @ w_proj_b[l])
                  + gates[:, :, 2] * (y_c @ w_proj_c[l]))
        x = x + (merged.astype(x.dtype) @ w_out[l]).astype(x.dtype)
    return x
```

```python
import functools

import jax
import jax.numpy as jnp
import numpy as np
from jax import lax
from jax.experimental import pallas as pl
from jax.experimental.pallas import tpu as pltpu

F32 = jnp.float32
BF16 = jnp.bfloat16

RMS_EPS = 1e-6
LNX_EPS = 64e-5
L2_EPS = 1e-12
ROPE_THETA = 10000.0
NEG_INF = -1e30

HEAD = 64
LANES = 128
CHUNK = 64
SW_BLOCK = 128
ROW_TILE = 256
VMEM_LIMIT = 56 * 1024 * 1024


def _dot(a, b):
    return jnp.dot(a, b, preferred_element_type=F32)


def _dot_nt(a, b):
    return lax.dot_general(a, b, (((1,), (1,)), ((), ())), preferred_element_type=F32)


def _dot_tn(a, b):
    return lax.dot_general(a, b, (((0,), (0,)), ((), ())), preferred_element_type=F32)


def _split2(x):
    hi = x.astype(BF16)
    lo = (x - hi.astype(F32)).astype(BF16)
    return hi, lo


def _split3(x):
    hi = x.astype(BF16)
    r1 = x - hi.astype(F32)
    mid = r1.astype(BF16)
    lo = (r1 - mid.astype(F32)).astype(BF16)
    return hi, mid, lo


def _dot_x3(a, b_hi, b_lo):
    a_hi, a_lo = _split2(a)
    return _dot(a_hi, b_hi) + (_dot(a_hi, b_lo) + _dot(a_lo, b_hi))


def _group_sum(x, ones_bd):
    hi, mid, lo = _split3(x)
    return _dot(hi, ones_bd) + (_dot(mid, ones_bd) + _dot(lo, ones_bd))


def _sigmoid(x):
    return 1.0 / (1.0 + jnp.exp(-x))


def _silu(x):
    return x * _sigmoid(x)


def _softplus(x):
    return jnp.maximum(x, 0.0) + jnp.log(1.0 + jnp.exp(-jnp.abs(x)))


def _block_diag2(y, lo_mask):
    return jnp.concatenate([jnp.where(lo_mask, y, 0.0), jnp.where(lo_mask, 0.0, y)], axis=0)


def _proj_in_kernel(x_ref, g_ref, w_ref, *out_refs, splits):
    x = x_ref[...]
    ms = jnp.mean(x * x, axis=-1, keepdims=True)
    h = (x * lax.rsqrt(ms + RMS_EPS) * g_ref[...]).astype(BF16)
    for o_ref, (lo, hi) in zip(out_refs, splits):
        o_ref[...] = _dot(h, w_ref[:, lo:hi])


def _proj_in(x2, g, w_bf, widths):
    n, d = x2.shape
    offs = np.concatenate([[0], np.cumsum(widths)])
    splits = tuple((int(offs[i]), int(offs[i + 1])) for i in range(len(widths)))
    return pl.pallas_call(
        functools.partial(_proj_in_kernel, splits=splits),
        grid=(n // ROW_TILE,),
        in_specs=[
            pl.BlockSpec((ROW_TILE, d), lambda i: (i, 0)),
            pl.BlockSpec((1, d), lambda i: (0, 0)),
            pl.BlockSpec(w_bf.shape, lambda i: (0, 0), pipeline_mode=pl.Buffered(1)),
        ],
        out_specs=[pl.BlockSpec((ROW_TILE, w), lambda i: (i, 0)) for w in widths],
        out_shape=[jax.ShapeDtypeStruct((n, w), F32) for w in widths],
        compiler_params=pltpu.CompilerParams(
            dimension_semantics=("arbitrary",), vmem_limit_bytes=VMEM_LIMIT),
        name="proj_in",
    )(x2, g, w_bf)


def _mem_kv_kernel(mem_ref, g_ref, w_ref, kg_ref, k_ref, v_ref, *, xw, hd):
    m = mem_ref[...]
    ms = jnp.mean(m * m, axis=-1, keepdims=True)
    h = (m * lax.rsqrt(ms + RMS_EPS) * g_ref[...]).astype(BF16)
    kv = _dot(h, w_ref[...])
    v_ref[...] = kv[:, xw:]
    for j in range(xw // hd):
        kj = kv[:, j * hd:(j + 1) * hd]
        msk = jnp.mean(kj * kj, axis=-1, keepdims=True)
        k_ref[:, j * hd:(j + 1) * hd] = kj * lax.rsqrt(msk + RMS_EPS) * kg_ref[...]


def _mem_kv(mem2, g, w_bf, kg, m_len, xw, hd):
    n, d = mem2.shape
    return pl.pallas_call(
        functools.partial(_mem_kv_kernel, xw=xw, hd=hd),
        grid=(n // m_len,),
        in_specs=[
            pl.BlockSpec((m_len, d), lambda i: (i, 0)),
            pl.BlockSpec((1, d), lambda i: (0, 0)),
            pl.BlockSpec(w_bf.shape, lambda i: (0, 0)),
            pl.BlockSpec((1, hd), lambda i: (0, 0)),
        ],
        out_specs=[pl.BlockSpec((m_len, xw), lambda i: (i, 0))] * 2,
        out_shape=[jax.ShapeDtypeStruct((n, xw), F32)] * 2,
        compiler_params=pltpu.CompilerParams(
            dimension_semantics=("arbitrary",), vmem_limit_bytes=VMEM_LIMIT),
        name="mem_kv",
    )(mem2, g, w_bf, kg)


def _rwkv_kernel(pa_ref, mu_ref, w0_ref, a0_ref, lora_hi_ref, lora_lo_ref, kk_ref, ka_ref, rk_ref,
                 lng_ref, lnb_ref, ones_ref, ya_ref,
                 carry_ref, s_ref, r_s, k_s, v_s, lw_s, n_s, b_s, y_s, *, width, lora2):
    tt = pa_ref.shape[0]
    n_pairs = width // LANES
    mixw = 3 * width + lora2

    @pl.when(pl.program_id(1) == 0)
    def _():
        carry_ref[...] = jnp.zeros_like(carry_ref)
        s_ref[...] = jnp.zeros_like(s_ref)

    u = pa_ref[:, 0:mixw]
    row = lax.broadcasted_iota(jnp.int32, (tt, 1), 0)
    prev = jnp.where(row == 0, carry_ref[...], pltpu.roll(u, 1, 0))
    carry_ref[...] = u[tt - 1:tt, :]
    mixed = u + (prev - u) * mu_ref[...]
    r = mixed[:, 0:width]
    k = mixed[:, width:2 * width]
    v = mixed[:, 2 * width:3 * width]
    lo_ra = mixed[:, 3 * width:mixw]
    lane_l = lax.broadcasted_iota(jnp.int32, (1, lora2), 1)
    lo_in = jnp.where(lane_l < lora2 // 2, jnp.tanh(lo_ra), lo_ra)
    proj = _dot_x3(lo_in, lora_hi_ref[...], lora_lo_ref[...])
    w_log = -_softplus(-(w0_ref[...] + proj[:, 0:width])) - 0.5
    a_sig = _sigmoid(a0_ref[...] + proj[:, width:2 * width])
    ones_bd = ones_ref[...]
    kk = k * kk_ref[...]
    nrm = jnp.sqrt(_group_sum(kk * kk, ones_bd))
    kkn = kk / jnp.maximum(nrm, L2_EPS)
    k2 = k * (1.0 + (a_sig - 1.0) * ka_ref[...])
    r_s[...] = r
    k_s[...] = k2
    v_s[...] = v
    lw_s[...] = -jnp.exp(w_log)
    n_s[...] = kkn
    b_s[...] = kkn * a_sig

    ti = lax.broadcasted_iota(jnp.int32, (CHUNK, LANES), 0)
    li = lax.broadcasted_iota(jnp.int32, (CHUNK, LANES), 1)
    lo_mask = li < HEAD
    lj = jnp.where(lo_mask, li, li - HEAD)
    strict = ti > lj
    incl = ti >= lj
    eye = (ti == lj).astype(F32)
    tri = (lax.broadcasted_iota(jnp.int32, (CHUNK, CHUNK), 0)
           >= lax.broadcasted_iota(jnp.int32, (CHUNK, CHUNK), 1)).astype(BF16)

    def chunk_body(c, carry):
        rows = pl.ds(pl.multiple_of(c * CHUNK, CHUNK), CHUNK)
        lw = lw_s[rows, :]
        l_hi, l_mid, l_lo = _split3(lw)
        cs = _dot(tri, l_hi) + (_dot(tri, l_mid) + _dot(tri, l_lo))
        cs_last = cs[CHUNK - 1:CHUNK, :]
        e_in = jnp.exp(cs)
        e_ex = jnp.exp(cs - lw)
        e_neg = jnp.exp(-cs)
        e_tail = jnp.exp(cs_last - cs)
        w_last = jnp.exp(cs_last)
        rc = r_s[rows, :]
        kc = k_s[rows, :]
        vc = v_s[rows, :]
        nc = n_s[rows, :]
        bc = b_s[rows, :]
        rt_all = rc * e_in
        at_all = -nc * e_ex
        kt_all = kc * e_neg
        bt_all = bc * e_neg
        kh_all = kc * e_tail
        bh_all = bc * e_tail
        for p in range(n_pairs):
            ls = slice(p * LANES, (p + 1) * LANES)
            vp = vc[:, ls]
            lhs2 = jnp.concatenate([at_all[:, ls], rt_all[:, ls]], axis=0).astype(BF16)
            gb = _dot_nt(lhs2, _block_diag2(bt_all[:, ls], lo_mask).astype(BF16))
            gk = _dot_nt(lhs2, _block_diag2(kt_all[:, ls], lo_mask).astype(BF16))
            a_ab = jnp.where(strict, gb[0:CHUNK], 0.0)
            a_rb = jnp.where(incl, gb[CHUNK:], 0.0)
            a_ak = jnp.where(strict, gk[0:CHUNK], 0.0)
            a_rk = jnp.where(incl, gk[CHUNK:], 0.0)
            x = eye + a_ab
            pw = _dot(a_ab.astype(BF16), _block_diag2(a_ab, lo_mask).astype(BF16))
            for _ in range(4):
                px = _dot(jnp.concatenate([pw, x], axis=0).astype(BF16),
                          _block_diag2(pw, lo_mask).astype(BF16))
                pw = px[0:CHUNK]
                x = x + px[CHUNK:]
            x = x + _dot(x.astype(BF16), _block_diag2(pw, lo_mask).astype(BF16))
            s_p = s_ref[p]
            asrs = _dot_nt(lhs2, _block_diag2(s_p, lo_mask).astype(BF16))
            v_bd = _block_diag2(vp, lo_mask).astype(BF16)
            z = asrs[0:CHUNK] + _dot(a_ak.astype(BF16), v_bd)
            u_p = _dot(x.astype(BF16), _block_diag2(z, lo_mask).astype(BF16))
            u_bd = _block_diag2(u_p, lo_mask).astype(BF16)
            y = asrs[CHUNK:] + _dot(jnp.concatenate([a_rb, a_rk], axis=1).astype(BF16),
                                    jnp.concatenate([u_bd, v_bd], axis=0))
            y_s[rows, ls] = y
            d = _dot_tn(jnp.concatenate([u_p, vp], axis=0).astype(BF16),
                        jnp.concatenate([bh_all[:, ls], kh_all[:, ls]], axis=0).astype(BF16))
            s_ref[p] = s_p * w_last[:, ls] + jnp.where(lo_mask, d[0:HEAD], d[HEAD:])
        return carry

    lax.fori_loop(0, tt // CHUNK, chunk_body, 0)

    y = y_s[...]
    inv_n = 1.0 / HEAD
    mean = _group_sum(y, ones_bd) * inv_n
    yc = y - mean
    var = _group_sum(yc * yc, ones_bd) * inv_n
    yn = yc * lax.rsqrt(var + LNX_EPS) * lng_ref[...] + lnb_ref[...]
    bonus = _group_sum(r_s[...] * k_s[...] * rk_ref[...], ones_bd) * v_s[...]
    ya_ref[...] = (yn + bonus) * _silu(pa_ref[:, mixw:mixw + width])


def _rwkv(pa, bsz, t, width, lora2, params):
    n = pa.shape[0]
    nt = t // ROW_TILE
    full = lambda a: pl.BlockSpec(a.shape, lambda b, i: (0,) * a.ndim)
    scr = pltpu.VMEM((ROW_TILE, width), F32)
    return pl.pallas_call(
        functools.partial(_rwkv_kernel, width=width, lora2=lora2),
        grid=(bsz, nt),
        in_specs=[pl.BlockSpec((ROW_TILE, pa.shape[1]), lambda b, i: (b * nt + i, 0))]
        + [full(a) for a in params],
        out_specs=pl.BlockSpec((ROW_TILE, width), lambda b, i: (b * nt + i, 0)),
        out_shape=jax.ShapeDtypeStruct((n, width), F32),
        scratch_shapes=[
            pltpu.VMEM((1, 3 * width + lora2), F32),
            pltpu.VMEM((width // LANES, HEAD, LANES), F32),
            scr, scr, scr, scr, scr, scr, scr,
        ],
        compiler_params=pltpu.CompilerParams(
            dimension_semantics=("arbitrary", "arbitrary"), vmem_limit_bytes=VMEM_LIMIT),
        name="rwkv",
    )(pa, *params)


def _attn_out_kernel(x_ref, pb_ref, pc_ref, pg_ref, ya_ref, pos_ref, km_ref, vm_ref,
                     inv_ref, qg_ref, kg_ref, sink_ref, xqg_ref, ones_ref,
                     wa_ref, wb_ref, wc_ref, wo_ref, o_ref, kprev_ref, vprev_ref, yb_s, yc_s,
                     *, sw_w, kv_w, x_w, x_hd):
    tm = x_ref.shape[0]
    d = x_ref.shape[1]
    first = pl.program_id(1) == 0
    ones_bd = ones_ref[...]

    @pl.when(first)
    def _():
        kprev_ref[...] = jnp.zeros_like(kprev_ref)
        vprev_ref[...] = jnp.zeros_like(vprev_ref)

    pos = pos_ref[...].astype(F32)
    ang = pos * inv_ref[...]
    cos_p = jnp.cos(ang)
    sin_p = jnp.sin(ang)

    def rope(xn):
        w = xn.shape[1]
        reps = w // LANES
        first_half = (lax.broadcasted_iota(jnp.int32, (1, w), 1) & (HEAD - 1)) < (HEAD // 2)
        cos_t = jnp.concatenate([cos_p] * reps, axis=1) if reps > 1 else cos_p
        sin_t = jnp.concatenate([sin_p] * reps, axis=1) if reps > 1 else sin_p
        rot = jnp.where(first_half, -pltpu.roll(xn, w - HEAD // 2, 1), pltpu.roll(xn, HEAD // 2, 1))
        return xn * cos_t + rot * sin_t

    q = pb_ref[:, 0:sw_w]
    qn = q * lax.rsqrt(_group_sum(q * q, ones_bd) * (1.0 / HEAD) + RMS_EPS) * qg_ref[...]
    qr = rope(qn) * (HEAD ** -0.5)
    kx = pb_ref[:, sw_w:sw_w + kv_w]
    kn = kx * lax.rsqrt(_group_sum(kx * kx, ones_bd[0:kv_w, 0:kv_w]) * (1.0 / HEAD) + RMS_EPS) * kg_ref[...]
    kr = rope(kn)
    vx = pb_ref[:, sw_w + kv_w:sw_w + 2 * kv_w]

    lo_mask = lax.broadcasted_iota(jnp.int32, (1, LANES), 1) < HEAD
    qi = lax.broadcasted_iota(jnp.int32, (SW_BLOCK, 2 * SW_BLOCK), 0)
    kj = lax.broadcasted_iota(jnp.int32, (SW_BLOCK, 2 * SW_BLOCK), 1)
    allowed = (kj > qi) & (kj <= qi + SW_BLOCK)
    n_sub = tm // SW_BLOCK
    n_kv = kv_w // HEAD
    pairs_per_kv = (sw_w // LANES) // n_kv
    for sb in range(n_sub):
        rs = slice(sb * SW_BLOCK, (sb + 1) * SW_BLOCK)
        if sb == 0:
            kband = jnp.concatenate([kprev_ref[...], kr[rs]], axis=0)
            vband = jnp.concatenate([vprev_ref[...], vx[rs]], axis=0)
            mask = allowed & (kj >= jnp.where(first, SW_BLOCK, 0))
        else:
            kband = kr[(sb - 1) * SW_BLOCK:(sb + 1) * SW_BLOCK]
            vband = vx[(sb - 1) * SW_BLOCK:(sb + 1) * SW_BLOCK]
            mask = allowed
        kband_sw = pltpu.roll(kband, HEAD, 1)
        vband_sw = pltpu.roll(vband, HEAD, 1)
        for g in range(n_kv):
            k_lo = kband if g == 0 else kband_sw
            k_hi = kband_sw if g == 0 else kband
            v_lo = vband if g == 0 else vband_sw
            v_hi = vband_sw if g == 0 else vband
            k_bd = jnp.concatenate([jnp.where(lo_mask, k_lo, 0.0), jnp.where(lo_mask, 0.0, k_hi)],
                                   axis=0).astype(BF16)
            v_bd = jnp.concatenate([jnp.where(lo_mask, v_lo, 0.0), jnp.where(lo_mask, 0.0, v_hi)],
                                   axis=0).astype(BF16)
            for pp in range(pairs_per_kv):
                p = g * pairs_per_kv + pp
                ls = slice(p * LANES, (p + 1) * LANES)
                s2 = _dot_nt(qr[rs, ls].astype(BF16), k_bd)
                probs = []
                for hh in range(2):
                    s = jnp.where(mask, s2[:, hh * 2 * SW_BLOCK:(hh + 1) * 2 * SW_BLOCK], NEG_INF)
                    sink = sink_ref[0, 2 * p + hh]
                    m = jnp.maximum(jnp.max(s, axis=-1, keepdims=True), sink)
                    e = jnp.exp(s - m)
                    den = jnp.sum(e, axis=-1, keepdims=True) + jnp.exp(sink - m)
                    probs.append((e / den).astype(BF16))
                yb_s[rs, ls] = _dot(jnp.concatenate(probs, axis=1), v_bd)
    kprev_ref[...] = kr[tm - SW_BLOCK:, :]
    vprev_ref[...] = vx[tm - SW_BLOCK:, :]
    y_b = (yb_s[...] * _silu(pb_ref[:, sw_w + 2 * kv_w:2 * sw_w + 2 * kv_w])).astype(BF16)

    for j in range(x_w // x_hd):
        ls = slice(j * x_hd, (j + 1) * x_hd)
        xq = pc_ref[:, ls]
        xqn = xq * lax.rsqrt(jnp.mean(xq * xq, axis=-1, keepdims=True) + RMS_EPS) * xqg_ref[...]
        s = _dot_nt((xqn * (x_hd ** -0.5)).astype(BF16), km_ref[:, ls].astype(BF16))
        m = jnp.max(s, axis=-1, keepdims=True)
        e = jnp.exp(s - m)
        pr = (e / jnp.sum(e, axis=-1, keepdims=True)).astype(BF16)
        yc_s[:, ls] = _dot(pr, vm_ref[:, ls].astype(BF16))
    y_c = (yc_s[...] * _silu(pc_ref[:, x_w:2 * x_w])).astype(BF16)

    y_a = ya_ref[...].astype(BF16)
    merged = _sigmoid(pg_ref[:, 0:d]) * _dot(y_a, wa_ref[...])
    merged = merged + _sigmoid(pg_ref[:, d:2 * d]) * _dot(y_b, wb_ref[...])
    merged = merged + _sigmoid(pg_ref[:, 2 * d:3 * d]) * _dot(y_c, wc_ref[...])
    o_ref[...] = x_ref[...] + _dot(merged.astype(BF16), wo_ref[...])


def _attn_out(x2, pb, pc, pg, ya, pos2, km, vm, params, bsz, t, m_len, sw_w, kv_w, x_w, x_hd):
    n, d = x2.shape
    nt = t // ROW_TILE
    row = lambda a: pl.BlockSpec((ROW_TILE, a.shape[1]), lambda b, i: (b * nt + i, 0))
    full = lambda a: pl.BlockSpec(a.shape, lambda b, i: (0,) * a.ndim)
    memspec = pl.BlockSpec((m_len, x_w), lambda b, i: (b, 0))
    in_specs = [row(x2), row(pb), row(pc), row(pg), row(ya), row(pos2), memspec, memspec]
    for a in params:
        in_specs.append(full(a))
    in_specs[8 + 3] = pl.BlockSpec(memory_space=pltpu.SMEM)
    return pl.pallas_call(
        functools.partial(_attn_out_kernel, sw_w=sw_w, kv_w=kv_w, x_w=x_w, x_hd=x_hd),
        grid=(bsz, nt),
        in_specs=in_specs,
        out_specs=pl.BlockSpec((ROW_TILE, d), lambda b, i: (b * nt + i, 0)),
        out_shape=jax.ShapeDtypeStruct((n, d), F32),
        scratch_shapes=[
            pltpu.VMEM((SW_BLOCK, kv_w), F32),
            pltpu.VMEM((SW_BLOCK, kv_w), F32),
            pltpu.VMEM((ROW_TILE, sw_w), F32),
            pltpu.VMEM((ROW_TILE, x_w), F32),
        ],
        compiler_params=pltpu.CompilerParams(
            dimension_semantics=("arbitrary", "arbitrary"), vmem_limit_bytes=VMEM_LIMIT),
        name="attn_out",
    )(x2, pb, pc, pg, ya, pos2, km, vm, *params)


def _ones_block_diag(width, block):
    i = np.arange(width)
    return jnp.asarray((i[:, None] // block) == (i[None, :] // block), dtype=BF16)


def kernel(x, mem, positions, norm_g, mem_norm_g, w_in, mu_rkv, mu_wa, w0, w2, a0, a2, k_k, k_a, r_k,
           lnx_g, lnx_b, q_norm_g, k_norm_g, sinks, xq_norm_g, xk_norm_g, w_mem_kv,
           w_proj_a, w_proj_b, w_proj_c, w_out):
    bsz, t, d = x.shape
    m_len = mem.shape[1]
    depth = w_in.shape[0]
    rw_w = w0.shape[1]
    lora = w2.shape[1]
    sw_w = w_proj_b.shape[1]
    x_w = w_proj_c.shape[1]
    x_hd = xq_norm_g.shape[1]
    kv_w = (w_in.shape[2] - (4 * rw_w + 2 * lora) - 2 * sw_w - 2 * x_w - 3 * d) // 2
    assert t % ROW_TILE == 0 and rw_w % LANES == 0 and sw_w % LANES == 0 and kv_w == LANES
    assert q_norm_g.shape[1] == HEAD and r_k.shape[2] == HEAD and 2 * lora == LANES
    widths = (4 * rw_w + 2 * lora, 2 * sw_w + 2 * kv_w, 2 * x_w, 3 * d)
    n = bsz * t
    x2 = x.reshape(n, d)
    mem2 = mem.reshape(bsz * m_len, d)
    pos2 = positions.reshape(n, 1)
    half = HEAD // 2
    inv = ROPE_THETA ** (-(jnp.arange(LANES) % half).astype(F32) / half)
    inv = inv.reshape(1, LANES)
    ones_bd = _ones_block_diag(rw_w, HEAD)
    for l in range(depth):
        row = lambda a: a[l].reshape(1, -1)
        pa, pb, pc, pg = _proj_in(x2, row(norm_g), w_in[l].astype(BF16), widths)
        km, vm = _mem_kv(mem2, row(mem_norm_g), w_mem_kv[l].astype(BF16), row(xk_norm_g), m_len, x_w, x_hd)
        mu = jnp.concatenate([mu_rkv[l].reshape(1, -1), mu_wa[l].reshape(1, -1)], axis=1)
        zeros = jnp.zeros((lora, rw_w), F32)
        lora_w = jnp.concatenate([jnp.concatenate([w2[l], zeros], axis=1),
                                  jnp.concatenate([zeros, a2[l]], axis=1)], axis=0)
        lora_hi = lora_w.astype(BF16)
        lora_lo = (lora_w - lora_hi.astype(F32)).astype(BF16)
        rw_params = (mu, row(w0), row(a0), lora_hi, lora_lo, row(k_k), row(k_a), row(r_k),
                     row(lnx_g), row(lnx_b), ones_bd)
        ya = _rwkv(pa, bsz, t, rw_w, 2 * lora, rw_params)
        at_params = (inv, jnp.tile(row(q_norm_g), (1, sw_w // HEAD)), jnp.tile(row(k_norm_g), (1, kv_w // HEAD)),
                     row(sinks), row(xq_norm_g), ones_bd,
                     w_proj_a[l].astype(BF16), w_proj_b[l].astype(BF16), w_proj_c[l].astype(BF16),
                     w_out[l].astype(BF16))
        x2 = _attn_out(x2, pb, pc, pg, ya, pos2, km, vm, at_params, bsz, t, m_len, sw_w, kv_w, x_w, x_hd)
    return x2.reshape(bsz, t, d)
```

```python
import functools

import jax
import jax.numpy as jnp
import numpy as np
from jax import lax
from jax.experimental import pallas as pl
from jax.experimental.pallas import tpu as pltpu

F32 = jnp.float32
BF16 = jnp.bfloat16

RMS_EPS = 1e-6
LNX_EPS = 64e-5
L2_EPS = 1e-12
ROPE_THETA = 10000.0
NEG_INF = -1e30

HEAD = 64
LANES = 128
CHUNK = 64
SW_BLOCK = 128
ROW_TILE = 256
VMEM_LIMIT = 56 * 1024 * 1024


def _dot(a, b):
    return jnp.dot(a, b, preferred_element_type=F32)


def _dot_nt(a, b):
    return lax.dot_general(a, b, (((1,), (1,)), ((), ())), preferred_element_type=F32)


def _dot_tn(a, b):
    return lax.dot_general(a, b, (((0,), (0,)), ((), ())), preferred_element_type=F32)


def _split2(x):
    hi = x.astype(BF16)
    lo = (x - hi.astype(F32)).astype(BF16)
    return hi, lo


def _split3(x):
    hi = x.astype(BF16)
    r1 = x - hi.astype(F32)
    mid = r1.astype(BF16)
    lo = (r1 - mid.astype(F32)).astype(BF16)
    return hi, mid, lo


def _dot_x3(a, b_hi, b_lo):
    a_hi, a_lo = _split2(a)
    return _dot(a_hi, b_hi) + (_dot(a_hi, b_lo) + _dot(a_lo, b_hi))


def _group_sum(x, ones_bd):
    hi, mid, lo = _split3(x)
    return _dot(hi, ones_bd) + (_dot(mid, ones_bd) + _dot(lo, ones_bd))


def _sigmoid(x):
    return 1.0 / (1.0 + jnp.exp(-x))


def _silu(x):
    return x * _sigmoid(x)


def _softplus(x):
    return jnp.maximum(x, 0.0) + jnp.log(1.0 + jnp.exp(-jnp.abs(x)))


def _block_diag2(y, lo_mask):
    return jnp.concatenate([jnp.where(lo_mask, y, 0.0), jnp.where(lo_mask, 0.0, y)], axis=0)


def _proj_in_kernel(x_ref, g_ref, w_ref, *out_refs, splits):
    x = x_ref[...]
    ms = jnp.mean(x * x, axis=-1, keepdims=True)
    h = (x * lax.rsqrt(ms + RMS_EPS) * g_ref[...]).astype(BF16)
    for o_ref, (lo, hi) in zip(out_refs, splits):
        o_ref[...] = _dot(h, w_ref[:, lo:hi])


def _proj_in(x2, g, w_bf, widths):
    n, d = x2.shape
    offs = np.concatenate([[0], np.cumsum(widths)])
    splits = tuple((int(offs[i]), int(offs[i + 1])) for i in range(len(widths)))
    return pl.pallas_call(
        functools.partial(_proj_in_kernel, splits=splits),
        grid=(n // ROW_TILE,),
        in_specs=[
            pl.BlockSpec((ROW_TILE, d), lambda i: (i, 0)),
            pl.BlockSpec((1, d), lambda i: (0, 0)),
            pl.BlockSpec(w_bf.shape, lambda i: (0, 0), pipeline_mode=pl.Buffered(1)),
        ],
        out_specs=[pl.BlockSpec((ROW_TILE, w), lambda i: (i, 0)) for w in widths],
        out_shape=[jax.ShapeDtypeStruct((n, w), F32) for w in widths],
        compiler_params=pltpu.CompilerParams(
            dimension_semantics=("arbitrary",), vmem_limit_bytes=VMEM_LIMIT),
        name="proj_in",
    )(x2, g, w_bf)


def _mem_kv_kernel(mem_ref, g_ref, w_ref, kg_ref, k_ref, v_ref, *, xw, hd):
    m = mem_ref[...]
    ms = jnp.mean(m * m, axis=-1, keepdims=True)
    h = (m * lax.rsqrt(ms + RMS_EPS) * g_ref[...]).astype(BF16)
    kv = _dot(h, w_ref[...])
    v_ref[...] = kv[:, xw:]
    for j in range(xw // hd):
        kj = kv[:, j * hd:(j + 1) * hd]
        msk = jnp.mean(kj * kj, axis=-1, keepdims=True)
        k_ref[:, j * hd:(j + 1) * hd] = kj * lax.rsqrt(msk + RMS_EPS) * kg_ref[...]


def _mem_kv(mem2, g, w_bf, kg, m_len, xw, hd):
    n, d = mem2.shape
    return pl.pallas_call(
        functools.partial(_mem_kv_kernel, xw=xw, hd=hd),
        grid=(n // m_len,),
        in_specs=[
            pl.BlockSpec((m_len, d), lambda i: (i, 0)),
            pl.BlockSpec((1, d), lambda i: (0, 0)),
            pl.BlockSpec(w_bf.shape, lambda i: (0, 0)),
            pl.BlockSpec((1, hd), lambda i: (0, 0)),
        ],
        out_specs=[pl.BlockSpec((m_len, xw), lambda i: (i, 0))] * 2,
        out_shape=[jax.ShapeDtypeStruct((n, xw), F32)] * 2,
        compiler_params=pltpu.CompilerParams(
            dimension_semantics=("arbitrary",), vmem_limit_bytes=VMEM_LIMIT),
        name="mem_kv",
    )(mem2, g, w_bf, kg)


def _rwkv_kernel(pa_ref, mu_ref, w0_ref, a0_ref, lora_hi_ref, lora_lo_ref, kk_ref, ka_ref, rk_ref,
                 lng_ref, lnb_ref, ones_ref, ya_ref, carry_ref, s_ref, *, width, lora2):
    nb, tt, _ = pa_ref.shape
    n_pairs = width // LANES
    mixw = 3 * width + lora2
    nrow = nb * CHUNK
    probs = [(b, p) for b in range(nb) for p in range(n_pairs)]

    @pl.when(pl.program_id(0) == 0)
    def _():
        carry_ref[...] = jnp.zeros_like(carry_ref)
        s_ref[...] = jnp.zeros_like(s_ref)

    ti = lax.broadcasted_iota(jnp.int32, (CHUNK, LANES), 0)
    li = lax.broadcasted_iota(jnp.int32, (CHUNK, LANES), 1)
    lo_mask = li < HEAD
    lj = jnp.where(lo_mask, li, li - HEAD)
    strict = ti > lj
    incl = ti >= lj
    eye = (ti == lj).astype(F32)
    ri = lax.broadcasted_iota(jnp.int32, (nrow, nrow), 0)
    ci = lax.broadcasted_iota(jnp.int32, (nrow, nrow), 1)
    tri = ((ri >= ci) & (jnp.bitwise_xor(ri, ci) < CHUNK)).astype(BF16)
    row_id = lax.broadcasted_iota(jnp.int32, (nrow, 1), 0)
    ones_bd = ones_ref[...]
    mu = mu_ref[...]
    lane_l = lax.broadcasted_iota(jnp.int32, (1, lora2), 1)

    def bd(y):
        return _block_diag2(y, lo_mask).astype(BF16)

    def chunk_body(c, carry):
        rows = pl.ds(pl.multiple_of(c * CHUNK, CHUNK), CHUNK)
        u = jnp.concatenate([pa_ref[b, rows, 0:mixw] for b in range(nb)], axis=0)
        prev = pltpu.roll(u, 1, 0)
        for b in range(nb):
            prev = jnp.where(row_id == b * CHUNK, carry_ref[b:b + 1, :], prev)
            carry_ref[b:b + 1, :] = u[(b + 1) * CHUNK - 1:(b + 1) * CHUNK, :]
        mixed = u + (prev - u) * mu
        r = mixed[:, 0:width]
        k = mixed[:, width:2 * width]
        v = mixed[:, 2 * width:3 * width]
        lo_ra = mixed[:, 3 * width:mixw]
        lo_in = jnp.where(lane_l < lora2 // 2, jnp.tanh(lo_ra), lo_ra)
        proj = _dot_x3(lo_in, lora_hi_ref[...], lora_lo_ref[...])
        w_log = -_softplus(-(w0_ref[...] + proj[:, 0:width])) - 0.5
        a_sig = _sigmoid(a0_ref[...] + proj[:, width:2 * width])
        lw = -jnp.exp(w_log)
        kk = k * kk_ref[...]
        kkn = kk / jnp.maximum(jnp.sqrt(_group_sum(kk * kk, ones_bd)), L2_EPS)
        k2 = k * (1.0 + (a_sig - 1.0) * ka_ref[...])
        bvec = kkn * a_sig
        l_hi, l_mid, l_lo = _split3(lw)
        cs = _dot(tri, l_hi) + (_dot(tri, l_mid) + _dot(tri, l_lo))
        cs_last = [cs[(b + 1) * CHUNK - 1:(b + 1) * CHUNK, :] for b in range(nb)]
        cs_end = jnp.concatenate([jnp.broadcast_to(cl, (CHUNK, width)) for cl in cs_last], axis=0)
        w_last = [jnp.exp(cl) for cl in cs_last]
        e_in = jnp.exp(cs)
        e_neg = jnp.exp(-cs)
        e_tail = jnp.exp(cs_end - cs)
        rt = r * e_in
        at = -kkn * jnp.exp(cs - lw)
        kt = k2 * e_neg
        bt = bvec * e_neg
        kh = k2 * e_tail
        bh = bvec * e_tail

        def blk(arr, i):
            b, p = probs[i]
            return arr[b * CHUNK:(b + 1) * CHUNK, p * LANES:(p + 1) * LANES]

        idx = range(len(probs))
        lhs2 = [jnp.concatenate([blk(at, i), blk(rt, i)], axis=0).astype(BF16) for i in idx]
        g = [_dot_nt(lhs2[i], jnp.concatenate([bd(blk(bt, i)), bd(blk(kt, i))], axis=0)) for i in idx]
        a_ab = [jnp.where(strict, g[i][0:CHUNK, 0:LANES], 0.0) for i in idx]
        a_ak = [jnp.where(strict, g[i][0:CHUNK, LANES:], 0.0) for i in idx]
        a_rb = [jnp.where(incl, g[i][CHUNK:, 0:LANES], 0.0) for i in idx]
        a_rk = [jnp.where(incl, g[i][CHUNK:, LANES:], 0.0) for i in idx]
        v_bd = [bd(blk(v, i)) for i in idx]
        av = [_dot(a_ak[i].astype(BF16), v_bd[i]) for i in idx]
        x = [eye + a_ab[i] for i in idx]
        pw = [_dot(a_ab[i].astype(BF16), bd(a_ab[i])) for i in idx]
        for _ in range(4):
            px = [_dot(jnp.concatenate([pw[i], x[i]], axis=0).astype(BF16), bd(pw[i])) for i in idx]
            pw = [px[i][0:CHUNK] for i in idx]
            x = [x[i] + px[i][CHUNK:] for i in idx]
        x = [x[i] + _dot(x[i].astype(BF16), bd(pw[i])) for i in idx]
        s_old = [s_ref[i] for i in idx]
        asrs = [_dot_nt(lhs2[i], bd(s_old[i])) for i in idx]
        z = [asrs[i][0:CHUNK] + av[i] for i in idx]
        u_p = [_dot(x[i].astype(BF16), bd(z[i])) for i in idx]
        y = [asrs[i][CHUNK:] + _dot(jnp.concatenate([a_rb[i], a_rk[i]], axis=1).astype(BF16),
                                    jnp.concatenate([bd(u_p[i]), v_bd[i]], axis=0)) for i in idx]
        d = [_dot_tn(jnp.concatenate([u_p[i], blk(v, i)], axis=0).astype(BF16),
                     jnp.concatenate([blk(bh, i), blk(kh, i)], axis=0).astype(BF16)) for i in idx]
        for i in idx:
            b, p = probs[i]
            s_ref[i] = (s_old[i] * w_last[b][:, p * LANES:(p + 1) * LANES]
                        + jnp.where(lo_mask, d[i][0:HEAD], d[i][HEAD:]))
        y_all = jnp.concatenate(
            [jnp.concatenate([y[b * n_pairs + p] for p in range(n_pairs)], axis=1) for b in range(nb)], axis=0)
        inv_n = 1.0 / HEAD
        yc = y_all - _group_sum(y_all, ones_bd) * inv_n
        var = _group_sum(yc * yc, ones_bd) * inv_n
        yn = yc * lax.rsqrt(var + LNX_EPS) * lng_ref[...] + lnb_ref[...]
        bonus = _group_sum(r * k2 * rk_ref[...], ones_bd) * v
        for b in range(nb):
            bs = slice(b * CHUNK, (b + 1) * CHUNK)
            ya_ref[b, rows, :] = (yn[bs] + bonus[bs]) * _silu(pa_ref[b, rows, mixw:mixw + width])
        return carry

    lax.fori_loop(0, tt // CHUNK, chunk_body, 0)


def _rwkv(pa, bsz, t, width, lora2, params):
    pa3 = pa.reshape(bsz, t, pa.shape[1])
    full = lambda a: pl.BlockSpec(a.shape, lambda i: (0,) * a.ndim)
    ya = pl.pallas_call(
        functools.partial(_rwkv_kernel, width=width, lora2=lora2),
        grid=(t // ROW_TILE,),
        in_specs=[pl.BlockSpec((bsz, ROW_TILE, pa.shape[1]), lambda i: (0, i, 0))]
        + [full(a) for a in params],
        out_specs=pl.BlockSpec((bsz, ROW_TILE, width), lambda i: (0, i, 0)),
        out_shape=jax.ShapeDtypeStruct((bsz, t, width), F32),
        scratch_shapes=[
            pltpu.VMEM((bsz, 3 * width + lora2), F32),
            pltpu.VMEM((bsz * (width // LANES), HEAD, LANES), F32),
        ],
        compiler_params=pltpu.CompilerParams(
            dimension_semantics=("arbitrary",), vmem_limit_bytes=VMEM_LIMIT),
        name="rwkv",
    )(pa3, *params)
    return ya.reshape(bsz * t, width)


def _attn_out_kernel(x_ref, pb_ref, pc_ref, pg_ref, ya_ref, pos_ref, km_ref, vm_ref,
                     inv_ref, qg_ref, kg_ref, sink_ref, xqg_ref, ones_ref,
                     wa_ref, wb_ref, wc_ref, wo_ref, o_ref, kprev_ref, vprev_ref, yb_s, yc_s,
                     *, sw_w, kv_w, x_w, x_hd):
    tm = x_ref.shape[0]
    d = x_ref.shape[1]
    first = pl.program_id(1) == 0
    ones_bd = ones_ref[...]

    @pl.when(first)
    def _():
        kprev_ref[...] = jnp.zeros_like(kprev_ref)
        vprev_ref[...] = jnp.zeros_like(vprev_ref)

    pos = pos_ref[...].astype(F32)
    ang = pos * inv_ref[...]
    cos_p = jnp.cos(ang)
    sin_p = jnp.sin(ang)

    def rope(xn):
        w = xn.shape[1]
        reps = w // LANES
        first_half = (lax.broadcasted_iota(jnp.int32, (1, w), 1) & (HEAD - 1)) < (HEAD // 2)
        cos_t = jnp.concatenate([cos_p] * reps, axis=1) if reps > 1 else cos_p
        sin_t = jnp.concatenate([sin_p] * reps, axis=1) if reps > 1 else sin_p
        rot = jnp.where(first_half, -pltpu.roll(xn, w - HEAD // 2, 1), pltpu.roll(xn, HEAD // 2, 1))
        return xn * cos_t + rot * sin_t

    q = pb_ref[:, 0:sw_w]
    qn = q * lax.rsqrt(_group_sum(q * q, ones_bd) * (1.0 / HEAD) + RMS_EPS) * qg_ref[...]
    qr = rope(qn) * (HEAD ** -0.5)
    kx = pb_ref[:, sw_w:sw_w + kv_w]
    kn = kx * lax.rsqrt(_group_sum(kx * kx, ones_bd[0:kv_w, 0:kv_w]) * (1.0 / HEAD) + RMS_EPS) * kg_ref[...]
    kr = rope(kn)
    vx = pb_ref[:, sw_w + kv_w:sw_w + 2 * kv_w]

    lo_mask = lax.broadcasted_iota(jnp.int32, (1, LANES), 1) < HEAD
    qi = lax.broadcasted_iota(jnp.int32, (SW_BLOCK, 2 * SW_BLOCK), 0)
    kj = lax.broadcasted_iota(jnp.int32, (SW_BLOCK, 2 * SW_BLOCK), 1)
    allowed = (kj > qi) & (kj <= qi + SW_BLOCK)
    n_sub = tm // SW_BLOCK
    n_kv = kv_w // HEAD
    pairs_per_kv = (sw_w // LANES) // n_kv
    for sb in range(n_sub):
        rs = slice(sb * SW_BLOCK, (sb + 1) * SW_BLOCK)
        if sb == 0:
            kband = jnp.concatenate([kprev_ref[...], kr[rs]], axis=0)
            vband = jnp.concatenate([vprev_ref[...], vx[rs]], axis=0)
            mask = allowed & (kj >= jnp.where(first, SW_BLOCK, 0))
        else:
            kband = kr[(sb - 1) * SW_BLOCK:(sb + 1) * SW_BLOCK]
            vband = vx[(sb - 1) * SW_BLOCK:(sb + 1) * SW_BLOCK]
            mask = allowed
        kband_sw = pltpu.roll(kband, HEAD, 1)
        vband_sw = pltpu.roll(vband, HEAD, 1)
        for g in range(n_kv):
            k_lo = kband if g == 0 else kband_sw
            k_hi = kband_sw if g == 0 else kband
            v_lo = vband if g == 0 else vband_sw
            v_hi = vband_sw if g == 0 else vband
            k_bd = jnp.concatenate([jnp.where(lo_mask, k_lo, 0.0), jnp.where(lo_mask, 0.0, k_hi)],
                                   axis=0).astype(BF16)
            v_bd = jnp.concatenate([jnp.where(lo_mask, v_lo, 0.0), jnp.where(lo_mask, 0.0, v_hi)],
                                   axis=0).astype(BF16)
            for pp in range(pairs_per_kv):
                p = g * pairs_per_kv + pp
                ls = slice(p * LANES, (p + 1) * LANES)
                s2 = _dot_nt(qr[rs, ls].astype(BF16), k_bd)
                probs = []
                for hh in range(2):
                    s = jnp.where(mask, s2[:, hh * 2 * SW_BLOCK:(hh + 1) * 2 * SW_BLOCK], NEG_INF)
                    sink = sink_ref[0, 2 * p + hh]
                    m = jnp.maximum(jnp.max(s, axis=-1, keepdims=True), sink)
                    e = jnp.exp(s - m)
                    den = jnp.sum(e, axis=-1, keepdims=True) + jnp.exp(sink - m)
                    probs.append((e / den).astype(BF16))
                yb_s[rs, ls] = _dot(jnp.concatenate(probs, axis=1), v_bd)
    kprev_ref[...] = kr[tm - SW_BLOCK:, :]
    vprev_ref[...] = vx[tm - SW_BLOCK:, :]
    y_b = (yb_s[...] * _silu(pb_ref[:, sw_w + 2 * kv_w:2 * sw_w + 2 * kv_w])).astype(BF16)

    for j in range(x_w // x_hd):
        ls = slice(j * x_hd, (j + 1) * x_hd)
        xq = pc_ref[:, ls]
        xqn = xq * lax.rsqrt(jnp.mean(xq * xq, axis=-1, keepdims=True) + RMS_EPS) * xqg_ref[...]
        s = _dot_nt((xqn * (x_hd ** -0.5)).astype(BF16), km_ref[:, ls].astype(BF16))
        m = jnp.max(s, axis=-1, keepdims=True)
        e = jnp.exp(s - m)
        pr = (e / jnp.sum(e, axis=-1, keepdims=True)).astype(BF16)
        yc_s[:, ls] = _dot(pr, vm_ref[:, ls].astype(BF16))
    y_c = (yc_s[...] * _silu(pc_ref[:, x_w:2 * x_w])).astype(BF16)

    y_a = ya_ref[...].astype(BF16)
    merged = _sigmoid(pg_ref[:, 0:d]) * _dot(y_a, wa_ref[...])
    merged = merged + _sigmoid(pg_ref[:, d:2 * d]) * _dot(y_b, wb_ref[...])
    merged = merged + _sigmoid(pg_ref[:, 2 * d:3 * d]) * _dot(y_c, wc_ref[...])
    o_ref[...] = x_ref[...] + _dot(merged.astype(BF16), wo_ref[...])


def _attn_out(x2, pb, pc, pg, ya, pos2, km, vm, params, bsz, t, m_len, sw_w, kv_w, x_w, x_hd):
    n, d = x2.shape
    nt = t // ROW_TILE
    row = lambda a: pl.BlockSpec((ROW_TILE, a.shape[1]), lambda b, i: (b * nt + i, 0))
    full = lambda a: pl.BlockSpec(a.shape, lambda b, i: (0,) * a.ndim)
    memspec = pl.BlockSpec((m_len, x_w), lambda b, i: (b, 0))
    in_specs = [row(x2), row(pb), row(pc), row(pg), row(ya), row(pos2), memspec, memspec]
    for a in params:
        in_specs.append(full(a))
    in_specs[8 + 3] = pl.BlockSpec(memory_space=pltpu.SMEM)
    return pl.pallas_call(
        functools.partial(_attn_out_kernel, sw_w=sw_w, kv_w=kv_w, x_w=x_w, x_hd=x_hd),
        grid=(bsz, nt),
        in_specs=in_specs,
        out_specs=pl.BlockSpec((ROW_TILE, d), lambda b, i: (b * nt + i, 0)),
        out_shape=jax.ShapeDtypeStruct((n, d), F32),
        scratch_shapes=[
            pltpu.VMEM((SW_BLOCK, kv_w), F32),
            pltpu.VMEM((SW_BLOCK, kv_w), F32),
            pltpu.VMEM((ROW_TILE, sw_w), F32),
            pltpu.VMEM((ROW_TILE, x_w), F32),
        ],
        compiler_params=pltpu.CompilerParams(
            dimension_semantics=("arbitrary", "arbitrary"), vmem_limit_bytes=VMEM_LIMIT),
        name="attn_out",
    )(x2, pb, pc, pg, ya, pos2, km, vm, *params)


def _ones_block_diag(width, block):
    i = np.arange(width)
    return jnp.asarray((i[:, None] // block) == (i[None, :] // block), dtype=BF16)


def kernel(x, mem, positions, norm_g, mem_norm_g, w_in, mu_rkv, mu_wa, w0, w2, a0, a2, k_k, k_a, r_k,
           lnx_g, lnx_b, q_norm_g, k_norm_g, sinks, xq_norm_g, xk_norm_g, w_mem_kv,
           w_proj_a, w_proj_b, w_proj_c, w_out):
    bsz, t, d = x.shape
    m_len = mem.shape[1]
    depth = w_in.shape[0]
    rw_w = w0.shape[1]
    lora = w2.shape[1]
    sw_w = w_proj_b.shape[1]
    x_w = w_proj_c.shape[1]
    x_hd = xq_norm_g.shape[1]
    kv_w = (w_in.shape[2] - (4 * rw_w + 2 * lora) - 2 * sw_w - 2 * x_w - 3 * d) // 2
    assert t % ROW_TILE == 0 and rw_w % LANES == 0 and sw_w % LANES == 0 and kv_w == LANES
    assert q_norm_g.shape[1] == HEAD and r_k.shape[2] == HEAD and 2 * lora == LANES
    widths = (4 * rw_w + 2 * lora, 2 * sw_w + 2 * kv_w, 2 * x_w, 3 * d)
    n = bsz * t
    x2 = x.reshape(n, d)
    mem2 = mem.reshape(bsz * m_len, d)
    pos2 = positions.reshape(n, 1)
    half = HEAD // 2
    inv = ROPE_THETA ** (-(jnp.arange(LANES) % half).astype(F32) / half)
    inv = inv.reshape(1, LANES)
    ones_bd = _ones_block_diag(rw_w, HEAD)
    for l in range(depth):
        row = lambda a: a[l].reshape(1, -1)
        pa, pb, pc, pg = _proj_in(x2, row(norm_g), w_in[l].astype(BF16), widths)
        km, vm = _mem_kv(mem2, row(mem_norm_g), w_mem_kv[l].astype(BF16), row(xk_norm_g), m_len, x_w, x_hd)
        mu = jnp.concatenate([mu_rkv[l].reshape(1, -1), mu_wa[l].reshape(1, -1)], axis=1)
        zeros = jnp.zeros((lora, rw_w), F32)
        lora_w = jnp.concatenate([jnp.concatenate([w2[l], zeros], axis=1),
                                  jnp.concatenate([zeros, a2[l]], axis=1)], axis=0)
        lora_hi = lora_w.astype(BF16)
        lora_lo = (lora_w - lora_hi.astype(F32)).astype(BF16)
        rw_params = (mu, row(w0), row(a0), lora_hi, lora_lo, row(k_k), row(k_a), row(r_k),
                     row(lnx_g), row(lnx_b), ones_bd)
        ya = _rwkv(pa, bsz, t, rw_w, 2 * lora, rw_params)
        at_params = (inv, jnp.tile(row(q_norm_g), (1, sw_w // HEAD)), jnp.tile(row(k_norm_g), (1, kv_w // HEAD)),
                     row(sinks), row(xq_norm_g), ones_bd,
                     w_proj_a[l].astype(BF16), w_proj_b[l].astype(BF16), w_proj_c[l].astype(BF16),
                     w_out[l].astype(BF16))
        x2 = _attn_out(x2, pb, pc, pg, ya, pos2, km, vm, at_params, bsz, t, m_len, sw_w, kv_w, x_w, x_hd)
    return x2.reshape(bsz, t, d)
```

```python
import functools
import math

import jax
import jax.numpy as jnp
import numpy as np
from jax import lax
from jax.experimental import pallas as pl
from jax.experimental.pallas import tpu as pltpu

F32 = jnp.float32
BF16 = jnp.bfloat16

RMS_EPS = 1e-6
LNX_EPS = 64e-5
L2_EPS = 1e-12
DECAY_SCALE = math.exp(-0.5)
ROPE_THETA = 10000.0
NEG_INF = -1e30

HEAD = 64
LANES = 128
CHUNK = 64
SW_BLOCK = 128
ROW_TILE = 256
VMEM_LIMIT = 56 * 1024 * 1024


def _dot(a, b):
    return jnp.dot(a, b, preferred_element_type=F32)


def _dot_nt(a, b):
    return lax.dot_general(a, b, (((1,), (1,)), ((), ())), preferred_element_type=F32)


def _dot_tn(a, b):
    return lax.dot_general(a, b, (((0,), (0,)), ((), ())), preferred_element_type=F32)


def _split2(x):
    hi = x.astype(BF16)
    lo = (x - hi.astype(F32)).astype(BF16)
    return hi, lo


def _split3(x):
    hi = x.astype(BF16)
    r1 = x - hi.astype(F32)
    mid = r1.astype(BF16)
    lo = (r1 - mid.astype(F32)).astype(BF16)
    return hi, mid, lo


def _dot_x3(a, b_hi, b_lo):
    a_hi, a_lo = _split2(a)
    return _dot(a_hi, b_hi) + (_dot(a_hi, b_lo) + _dot(a_lo, b_hi))


def _head_sums(x):
    lo_mask = lax.broadcasted_iota(jnp.int32, (1, LANES), 1) < HEAD
    outs = []
    for p in range(x.shape[1] // LANES):
        xs = x[:, p * LANES:(p + 1) * LANES]
        s_lo = jnp.sum(jnp.where(lo_mask, xs, 0.0), axis=-1, keepdims=True)
        s_hi = jnp.sum(jnp.where(lo_mask, 0.0, xs), axis=-1, keepdims=True)
        outs.append(jnp.where(lo_mask, s_lo, s_hi))
    return outs[0] if len(outs) == 1 else jnp.concatenate(outs, axis=1)


def _sigmoid(x):
    return 1.0 / (1.0 + jnp.exp(-x))


def _silu(x):
    return x * _sigmoid(x)


def _block_diag2(y, lo_mask):
    return jnp.concatenate([jnp.where(lo_mask, y, 0.0), jnp.where(lo_mask, 0.0, y)], axis=0)


def _proj_in_kernel(x_ref, g_ref, w_ref, *out_refs, splits):
    x = x_ref[...]
    ms = jnp.mean(x * x, axis=-1, keepdims=True)
    h = (x * lax.rsqrt(ms + RMS_EPS) * g_ref[...]).astype(BF16)
    for o_ref, (lo, hi) in zip(out_refs, splits):
        o_ref[...] = _dot(h, w_ref[:, lo:hi])


def _proj_in(x2, g, w_bf, widths):
    n, d = x2.shape
    offs = np.concatenate([[0], np.cumsum(widths)])
    splits = tuple((int(offs[i]), int(offs[i + 1])) for i in range(len(widths)))
    return pl.pallas_call(
        functools.partial(_proj_in_kernel, splits=splits),
        grid=(n // ROW_TILE,),
        in_specs=[
            pl.BlockSpec((ROW_TILE, d), lambda i: (i, 0)),
            pl.BlockSpec((1, d), lambda i: (0, 0)),
            pl.BlockSpec(w_bf.shape, lambda i: (0, 0), pipeline_mode=pl.Buffered(1)),
        ],
        out_specs=[pl.BlockSpec((ROW_TILE, w), lambda i: (i, 0)) for w in widths],
        out_shape=[jax.ShapeDtypeStruct((n, w), F32) for w in widths],
        compiler_params=pltpu.CompilerParams(
            dimension_semantics=("arbitrary",), vmem_limit_bytes=VMEM_LIMIT),
        name="proj_in",
    )(x2, g, w_bf)


def _mem_kv_kernel(mem_ref, g_ref, w_ref, kg_ref, k_ref, v_ref, *, xw, hd):
    m = mem_ref[...]
    ms = jnp.mean(m * m, axis=-1, keepdims=True)
    h = (m * lax.rsqrt(ms + RMS_EPS) * g_ref[...]).astype(BF16)
    kv = _dot(h, w_ref[...])
    v_ref[...] = kv[:, xw:]
    for j in range(xw // hd):
        kj = kv[:, j * hd:(j + 1) * hd]
        msk = jnp.mean(kj * kj, axis=-1, keepdims=True)
        k_ref[:, j * hd:(j + 1) * hd] = kj * lax.rsqrt(msk + RMS_EPS) * kg_ref[...]


def _mem_kv(mem2, g, w_bf, kg, m_len, xw, hd):
    n, d = mem2.shape
    return pl.pallas_call(
        functools.partial(_mem_kv_kernel, xw=xw, hd=hd),
        grid=(n // m_len,),
        in_specs=[
            pl.BlockSpec((m_len, d), lambda i: (i, 0)),
            pl.BlockSpec((1, d), lambda i: (0, 0)),
            pl.BlockSpec(w_bf.shape, lambda i: (0, 0)),
            pl.BlockSpec((1, hd), lambda i: (0, 0)),
        ],
        out_specs=[pl.BlockSpec((m_len, xw), lambda i: (i, 0))] * 2,
        out_shape=[jax.ShapeDtypeStruct((n, xw), F32)] * 2,
        compiler_params=pltpu.CompilerParams(
            dimension_semantics=("arbitrary",), vmem_limit_bytes=VMEM_LIMIT),
        name="mem_kv",
    )(mem2, g, w_bf, kg)


def _rwkv_kernel(pa_ref, mu_ref, w0_ref, a0_ref, lora_hi_ref, lora_lo_ref, kk_ref, ka_ref, rk_ref,
                 lng_ref, lnb_ref, ya_ref, carry_ref, s_ref, *, width, lora2):
    nb, tt, _ = pa_ref.shape
    n_pairs = width // LANES
    mixw = 3 * width + lora2
    nrow = nb * CHUNK
    probs = [(b, p) for b in range(nb) for p in range(n_pairs)]

    @pl.when(pl.program_id(0) == 0)
    def _():
        carry_ref[...] = jnp.zeros_like(carry_ref)
        s_ref[...] = jnp.zeros_like(s_ref)

    ti = lax.broadcasted_iota(jnp.int32, (CHUNK, LANES), 0)
    li = lax.broadcasted_iota(jnp.int32, (CHUNK, LANES), 1)
    lo_mask = li < HEAD
    lj = jnp.where(lo_mask, li, li - HEAD)
    strict = ti > lj
    incl = ti >= lj
    eye = (ti == lj).astype(F32)
    ri = lax.broadcasted_iota(jnp.int32, (nrow, nrow), 0)
    ci = lax.broadcasted_iota(jnp.int32, (nrow, nrow), 1)
    tri = ((ri >= ci) & (jnp.bitwise_xor(ri, ci) < CHUNK)).astype(BF16)
    row_id = lax.broadcasted_iota(jnp.int32, (nrow, 1), 0)
    mu = mu_ref[...]
    lane_l = lax.broadcasted_iota(jnp.int32, (1, lora2), 1)

    def bd(y):
        return _block_diag2(y, lo_mask).astype(BF16)

    def chunk_body(c, carry):
        rows = pl.ds(pl.multiple_of(c * CHUNK, CHUNK), CHUNK)
        u = jnp.concatenate([pa_ref[b, rows, 0:mixw] for b in range(nb)], axis=0)
        prev = pltpu.roll(u, 1, 0)
        for b in range(nb):
            prev = jnp.where(row_id == b * CHUNK, carry_ref[b:b + 1, :], prev)
            carry_ref[b:b + 1, :] = u[(b + 1) * CHUNK - 1:(b + 1) * CHUNK, :]
        mixed = u + (prev - u) * mu
        r = mixed[:, 0:width]
        k = mixed[:, width:2 * width]
        v = mixed[:, 2 * width:3 * width]
        lo_ra = mixed[:, 3 * width:mixw]
        lo_in = jnp.where(lane_l < lora2 // 2, jnp.tanh(lo_ra), lo_ra)
        proj = _dot_x3(lo_in, lora_hi_ref[...], lora_lo_ref[...])
        lw = -DECAY_SCALE * _sigmoid(w0_ref[...] + proj[:, 0:width])
        a_sig = _sigmoid(a0_ref[...] + proj[:, width:2 * width])
        kk = k * kk_ref[...]
        kkn = kk * lax.rsqrt(jnp.maximum(_head_sums(kk * kk), L2_EPS * L2_EPS))
        k2 = k * (1.0 + (a_sig - 1.0) * ka_ref[...])
        bvec = kkn * a_sig
        l_hi, l_mid, l_lo = _split3(lw)
        cs = _dot(tri, l_hi) + (_dot(tri, l_mid) + _dot(tri, l_lo))
        cs_last = [cs[(b + 1) * CHUNK - 1:(b + 1) * CHUNK, :] for b in range(nb)]
        cs_end = jnp.concatenate([jnp.broadcast_to(cl, (CHUNK, width)) for cl in cs_last], axis=0)
        w_last = [jnp.exp(cl) for cl in cs_last]
        e_in = jnp.exp(cs)
        e_neg = jnp.exp(-cs)
        e_tail = jnp.exp(cs_end - cs)
        rt = r * e_in
        at = -kkn * jnp.exp(cs - lw)
        kt = k2 * e_neg
        bt = bvec * e_neg
        kh = k2 * e_tail
        bh = bvec * e_tail

        def blk(arr, i):
            b, p = probs[i]
            return arr[b * CHUNK:(b + 1) * CHUNK, p * LANES:(p + 1) * LANES]

        idx = range(len(probs))
        lhs2 = [jnp.concatenate([blk(at, i), blk(rt, i)], axis=0).astype(BF16) for i in idx]
        g = [_dot_nt(lhs2[i], jnp.concatenate([bd(blk(bt, i)), bd(blk(kt, i))], axis=0)) for i in idx]
        a_ab = [jnp.where(strict, g[i][0:CHUNK, 0:LANES], 0.0) for i in idx]
        a_ak = [jnp.where(strict, g[i][0:CHUNK, LANES:], 0.0) for i in idx]
        a_rb = [jnp.where(incl, g[i][CHUNK:, 0:LANES], 0.0) for i in idx]
        a_rk = [jnp.where(incl, g[i][CHUNK:, LANES:], 0.0) for i in idx]
        v_bd = [bd(blk(v, i)) for i in idx]
        av = [_dot(a_ak[i].astype(BF16), v_bd[i]) for i in idx]
        x = [eye + a_ab[i] for i in idx]
        pw = [_dot(a_ab[i].astype(BF16), bd(a_ab[i])) for i in idx]
        for _ in range(4):
            px = [_dot(jnp.concatenate([pw[i], x[i]], axis=0).astype(BF16), bd(pw[i])) for i in idx]
            pw = [px[i][0:CHUNK] for i in idx]
            x = [x[i] + px[i][CHUNK:] for i in idx]
        x = [x[i] + _dot(x[i].astype(BF16), bd(pw[i])) for i in idx]
        s_old = [s_ref[i] for i in idx]
        asrs = [_dot_nt(lhs2[i], bd(s_old[i])) for i in idx]
        z = [asrs[i][0:CHUNK] + av[i] for i in idx]
        u_p = [_dot(x[i].astype(BF16), bd(z[i])) for i in idx]
        y = [asrs[i][CHUNK:] + _dot(jnp.concatenate([a_rb[i], a_rk[i]], axis=1).astype(BF16),
                                    jnp.concatenate([bd(u_p[i]), v_bd[i]], axis=0)) for i in idx]
        d = [_dot_tn(jnp.concatenate([u_p[i], blk(v, i)], axis=0).astype(BF16),
                     jnp.concatenate([blk(bh, i), blk(kh, i)], axis=0).astype(BF16)) for i in idx]
        for i in idx:
            b, p = probs[i]
            s_ref[i] = (s_old[i] * w_last[b][:, p * LANES:(p + 1) * LANES]
                        + jnp.where(lo_mask, d[i][0:HEAD], d[i][HEAD:]))
        y_all = jnp.concatenate(
            [jnp.concatenate([y[b * n_pairs + p] for p in range(n_pairs)], axis=1) for b in range(nb)], axis=0)
        inv_n = 1.0 / HEAD
        yc = y_all - _head_sums(y_all) * inv_n
        var = _head_sums(yc * yc) * inv_n
        yn = yc * lax.rsqrt(var + LNX_EPS) * lng_ref[...] + lnb_ref[...]
        bonus = _head_sums(r * k2 * rk_ref[...]) * v
        for b in range(nb):
            bs = slice(b * CHUNK, (b + 1) * CHUNK)
            ya_ref[b, rows, :] = (yn[bs] + bonus[bs]) * _silu(pa_ref[b, rows, mixw:mixw + width])
        return carry

    lax.fori_loop(0, tt // CHUNK, chunk_body, 0)


def _rwkv(pa, bsz, t, width, lora2, params):
    pa3 = pa.reshape(bsz, t, pa.shape[1])
    full = lambda a: pl.BlockSpec(a.shape, lambda i: (0,) * a.ndim)
    ya = pl.pallas_call(
        functools.partial(_rwkv_kernel, width=width, lora2=lora2),
        grid=(t // ROW_TILE,),
        in_specs=[pl.BlockSpec((bsz, ROW_TILE, pa.shape[1]), lambda i: (0, i, 0))]
        + [full(a) for a in params],
        out_specs=pl.BlockSpec((bsz, ROW_TILE, width), lambda i: (0, i, 0)),
        out_shape=jax.ShapeDtypeStruct((bsz, t, width), F32),
        scratch_shapes=[
            pltpu.VMEM((bsz, 3 * width + lora2), F32),
            pltpu.VMEM((bsz * (width // LANES), HEAD, LANES), F32),
        ],
        compiler_params=pltpu.CompilerParams(
            dimension_semantics=("arbitrary",), vmem_limit_bytes=VMEM_LIMIT),
        name="rwkv",
    )(pa3, *params)
    return ya.reshape(bsz * t, width)


def _attn_out_kernel(x_ref, pb_ref, pc_ref, pg_ref, ya_ref, pos_ref, km_ref, vm_ref,
                     inv_ref, qg_ref, kg_ref, sink_ref, xqg_ref,
                     wa_ref, wb_ref, wc_ref, wo_ref, o_ref, kprev_ref, vprev_ref,
                     *, sw_w, kv_w, x_w, x_hd):
    tm = x_ref.shape[0]
    d = x_ref.shape[1]
    first = pl.program_id(1) == 0

    @pl.when(first)
    def _():
        kprev_ref[...] = jnp.zeros_like(kprev_ref)
        vprev_ref[...] = jnp.zeros_like(vprev_ref)


    half = HEAD // 2
    n_grp = LANES // half
    qrows = tm // n_grp
    pos = pos_ref[...].astype(F32)
    lane_grp = lax.broadcasted_iota(jnp.int32, (1, LANES), 1) // half
    pos_d = pos[0:qrows]
    for j in range(1, n_grp):
        pos_d = jnp.where(lane_grp == j, pos[j * qrows:(j + 1) * qrows], pos_d)
    ang_d = pos_d * inv_ref[...]
    cos_d = jnp.cos(ang_d)
    sin_d = jnp.sin(ang_d)

    def spread(tab):
        rolled = [tab] + [pltpu.roll(tab, half * k, 1) for k in range(1, n_grp)]
        quarters = []
        for j in range(n_grp):
            cj = rolled[(0 - j) % n_grp]
            for gl in range(1, n_grp):
                cj = jnp.where(lane_grp == gl, rolled[(gl - j) % n_grp], cj)
            quarters.append(cj)
        return jnp.concatenate(quarters, axis=0)

    cos_p = spread(cos_d)
    sin_p = spread(sin_d)

    def rope(xn):
        w = xn.shape[1]
        reps = w // LANES
        first_half = (lax.broadcasted_iota(jnp.int32, (1, w), 1) & (HEAD - 1)) < half
        cos_t = jnp.concatenate([cos_p] * reps, axis=1) if reps > 1 else cos_p
        sin_t = jnp.concatenate([sin_p] * reps, axis=1) if reps > 1 else sin_p
        rot = jnp.where(first_half, -pltpu.roll(xn, w - half, 1), pltpu.roll(xn, half, 1))
        return xn * cos_t + rot * sin_t

    q = pb_ref[:, 0:sw_w]
    qn = q * lax.rsqrt(_head_sums(q * q) * (1.0 / HEAD) + RMS_EPS) * (qg_ref[...] * (HEAD ** -0.5))
    qr = rope(qn).astype(BF16)
    kx = pb_ref[:, sw_w:sw_w + kv_w]
    kn = kx * lax.rsqrt(_head_sums(kx * kx) * (1.0 / HEAD) + RMS_EPS) * kg_ref[...]
    kr = rope(kn)
    vx = pb_ref[:, sw_w + kv_w:sw_w + 2 * kv_w]
    n_xh = x_w // x_hd
    xqn = []
    for j in range(n_xh):
        xq = pc_ref[:, j * x_hd:(j + 1) * x_hd]
        xqn.append((xq * lax.rsqrt(jnp.mean(xq * xq, axis=-1, keepdims=True) + RMS_EPS)
                    * (xqg_ref[...] * (x_hd ** -0.5))).astype(BF16))

    lo_mask = lax.broadcasted_iota(jnp.int32, (1, LANES), 1) < HEAD
    qi = lax.broadcasted_iota(jnp.int32, (SW_BLOCK, 2 * SW_BLOCK), 0)
    kj = lax.broadcasted_iota(jnp.int32, (SW_BLOCK, 2 * SW_BLOCK), 1)
    allowed = (kj > qi) & (kj <= qi + SW_BLOCK)
    n_sub = tm // SW_BLOCK
    n_kv = kv_w // HEAD
    n_pairs = sw_w // LANES
    pairs_per_kv = n_pairs // n_kv

    s_c = [_dot_nt(xqn[j], km_ref[:, j * x_hd:(j + 1) * x_hd].astype(BF16)) for j in range(n_xh)]
    v_bds, masks, s_b = [], [], []
    for sb in range(n_sub):
        rs = slice(sb * SW_BLOCK, (sb + 1) * SW_BLOCK)
        if sb == 0:
            kband = jnp.concatenate([kprev_ref[...], kr[rs]], axis=0)
            vband = jnp.concatenate([vprev_ref[...], vx[rs]], axis=0)
            masks.append(allowed & (kj >= jnp.where(first, SW_BLOCK, 0)))
        else:
            kband = kr[(sb - 1) * SW_BLOCK:(sb + 1) * SW_BLOCK]
            vband = vx[(sb - 1) * SW_BLOCK:(sb + 1) * SW_BLOCK]
            masks.append(allowed)
        kband_sw = pltpu.roll(kband, HEAD, 1)
        vband_sw = pltpu.roll(vband, HEAD, 1)
        for g in range(n_kv):
            k_lo = kband if g == 0 else kband_sw
            k_hi = kband_sw if g == 0 else kband
            v_lo = vband if g == 0 else vband_sw
            v_hi = vband_sw if g == 0 else vband
            k_bd = jnp.concatenate([jnp.where(lo_mask, k_lo, 0.0), jnp.where(lo_mask, 0.0, k_hi)],
                                   axis=0).astype(BF16)
            v_bds.append(jnp.concatenate([jnp.where(lo_mask, v_lo, 0.0), jnp.where(lo_mask, 0.0, v_hi)],
                                         axis=0).astype(BF16))
            for pp in range(pairs_per_kv):
                p = g * pairs_per_kv + pp
                s_b.append(_dot_nt(qr[rs, p * LANES:(p + 1) * LANES], k_bd))
    kprev_ref[...] = kr[tm - SW_BLOCK:, :]
    vprev_ref[...] = vx[tm - SW_BLOCK:, :]

    proj_a = _dot(ya_ref[...].astype(BF16), wa_ref[...])

    p_c = []
    for j in range(n_xh):
        m = jnp.max(s_c[j], axis=-1, keepdims=True)
        e = jnp.exp(s_c[j] - m)
        p_c.append((e / jnp.sum(e, axis=-1, keepdims=True)).astype(BF16))
    p_b = []
    for sb in range(n_sub):
        for p in range(n_pairs):
            s2 = s_b[sb * n_pairs + p]
            probs = []
            for hh in range(2):
                s = jnp.where(masks[sb], s2[:, hh * 2 * SW_BLOCK:(hh + 1) * 2 * SW_BLOCK], NEG_INF)
                sink = sink_ref[0, 2 * p + hh]
                m = jnp.maximum(jnp.max(s, axis=-1, keepdims=True), sink)
                e = jnp.exp(s - m)
                den = jnp.sum(e, axis=-1, keepdims=True) + jnp.exp(sink - m)
                probs.append((e / den).astype(BF16))
            p_b.append(jnp.concatenate(probs, axis=1))
    o_c = [_dot(p_c[j], vm_ref[:, j * x_hd:(j + 1) * x_hd].astype(BF16)) for j in range(n_xh)]
    o_b = [[_dot(p_b[sb * n_pairs + p], v_bds[sb * n_kv + p // pairs_per_kv]) for p in range(n_pairs)]
           for sb in range(n_sub)]

    y_c = (jnp.concatenate(o_c, axis=1) * _silu(pc_ref[:, x_w:2 * x_w])).astype(BF16)
    proj_c = _dot(y_c, wc_ref[...])
    y_b = jnp.concatenate([jnp.concatenate(o_b[sb], axis=1) for sb in range(n_sub)], axis=0)
    y_b = (y_b * _silu(pb_ref[:, sw_w + 2 * kv_w:2 * sw_w + 2 * kv_w])).astype(BF16)
    proj_b = _dot(y_b, wb_ref[...])
    merged = _sigmoid(pg_ref[:, 0:d]) * proj_a
    merged = merged + _sigmoid(pg_ref[:, 2 * d:3 * d]) * proj_c
    merged = merged + _sigmoid(pg_ref[:, d:2 * d]) * proj_b
    o_ref[...] = x_ref[...] + _dot(merged.astype(BF16), wo_ref[...])


def _attn_out(x2, pb, pc, pg, ya, pos2, km, vm, params, bsz, t, m_len, sw_w, kv_w, x_w, x_hd):
    n, d = x2.shape
    nt = t // ROW_TILE
    row = lambda a: pl.BlockSpec((ROW_TILE, a.shape[1]), lambda b, i: (b * nt + i, 0))
    full = lambda a: pl.BlockSpec(a.shape, lambda b, i: (0,) * a.ndim)
    memspec = pl.BlockSpec((m_len, x_w), lambda b, i: (b, 0))
    in_specs = [row(x2), row(pb), row(pc), row(pg), row(ya), row(pos2), memspec, memspec]
    for a in params:
        in_specs.append(full(a))
    in_specs[8 + 3] = pl.BlockSpec(memory_space=pltpu.SMEM)
    return pl.pallas_call(
        functools.partial(_attn_out_kernel, sw_w=sw_w, kv_w=kv_w, x_w=x_w, x_hd=x_hd),
        grid=(bsz, nt),
        in_specs=in_specs,
        out_specs=pl.BlockSpec((ROW_TILE, d), lambda b, i: (b * nt + i, 0)),
        out_shape=jax.ShapeDtypeStruct((n, d), F32),
        scratch_shapes=[
            pltpu.VMEM((SW_BLOCK, kv_w), F32),
            pltpu.VMEM((SW_BLOCK, kv_w), F32),
        ],
        compiler_params=pltpu.CompilerParams(
            dimension_semantics=("arbitrary", "arbitrary"), vmem_limit_bytes=VMEM_LIMIT),
        name="attn_out",
    )(x2, pb, pc, pg, ya, pos2, km, vm, *params)


def kernel(x, mem, positions, norm_g, mem_norm_g, w_in, mu_rkv, mu_wa, w0, w2, a0, a2, k_k, k_a, r_k,
           lnx_g, lnx_b, q_norm_g, k_norm_g, sinks, xq_norm_g, xk_norm_g, w_mem_kv,
           w_proj_a, w_proj_b, w_proj_c, w_out):
    bsz, t, d = x.shape
    m_len = mem.shape[1]
    depth = w_in.shape[0]
    rw_w = w0.shape[1]
    lora = w2.shape[1]
    sw_w = w_proj_b.shape[1]
    x_w = w_proj_c.shape[1]
    x_hd = xq_norm_g.shape[1]
    kv_w = (w_in.shape[2] - (4 * rw_w + 2 * lora) - 2 * sw_w - 2 * x_w - 3 * d) // 2
    assert t % ROW_TILE == 0 and rw_w % LANES == 0 and sw_w % LANES == 0 and kv_w == LANES
    assert q_norm_g.shape[1] == HEAD and r_k.shape[2] == HEAD and 2 * lora == LANES
    widths = (4 * rw_w + 2 * lora, 2 * sw_w + 2 * kv_w, 2 * x_w, 3 * d)
    n = bsz * t
    x2 = x.reshape(n, d)
    mem2 = mem.reshape(bsz * m_len, d)
    pos2 = positions.reshape(n, 1)
    half = HEAD // 2
    inv = ROPE_THETA ** (-(jnp.arange(LANES) % half).astype(F32) / half)
    inv = inv.reshape(1, LANES)
    for l in range(depth):
        row = lambda a: a[l].reshape(1, -1)
        pa, pb, pc, pg = _proj_in(x2, row(norm_g), w_in[l].astype(BF16), widths)
        km, vm = _mem_kv(mem2, row(mem_norm_g), w_mem_kv[l].astype(BF16), row(xk_norm_g), m_len, x_w, x_hd)
        mu = jnp.concatenate([mu_rkv[l].reshape(1, -1), mu_wa[l].reshape(1, -1)], axis=1)
        zeros = jnp.zeros((lora, rw_w), F32)
        lora_w = jnp.concatenate([jnp.concatenate([w2[l], zeros], axis=1),
                                  jnp.concatenate([zeros, a2[l]], axis=1)], axis=0)
        lora_hi = lora_w.astype(BF16)
        lora_lo = (lora_w - lora_hi.astype(F32)).astype(BF16)
        rw_params = (mu, row(w0), row(a0), lora_hi, lora_lo, row(k_k), row(k_a), row(r_k),
                     row(lnx_g), row(lnx_b))
        ya = _rwkv(pa, bsz, t, rw_w, 2 * lora, rw_params)
        at_params = (inv, jnp.tile(row(q_norm_g), (1, sw_w // HEAD)), jnp.tile(row(k_norm_g), (1, kv_w // HEAD)),
                     row(sinks), row(xq_norm_g),
                     w_proj_a[l].astype(BF16), w_proj_b[l].astype(BF16), w_proj_c[l].astype(BF16),
                     w_out[l].astype(BF16))
        x2 = _attn_out(x2, pb, pc, pg, ya, pos2, km, vm, at_params, bsz, t, m_len, sw_w, kv_w, x_w, x_hd)
    return x2.reshape(bsz, t, d)
```

```python
import functools
import math

import jax
import jax.numpy as jnp
import numpy as np
from jax import lax
from jax.experimental import pallas as pl
from jax.experimental.pallas import tpu as pltpu

F32 = jnp.float32
BF16 = jnp.bfloat16

RMS_EPS = 1e-6
LNX_EPS = 64e-5
L2_EPS = 1e-12
DECAY_SCALE = math.exp(-0.5)
ROPE_THETA = 10000.0
NEG_INF = -1e30

HEAD = 64
LANES = 128
CHUNK = 64
SW_BLOCK = 128
ROW_TILE = 256
VMEM_LIMIT = 56 * 1024 * 1024


def _dot(a, b):
    return jnp.dot(a, b, preferred_element_type=F32)


def _dot_nt(a, b):
    return lax.dot_general(a, b, (((1,), (1,)), ((), ())), preferred_element_type=F32)


def _dot_tn(a, b):
    return lax.dot_general(a, b, (((0,), (0,)), ((), ())), preferred_element_type=F32)


def _split2(x):
    hi = x.astype(BF16)
    lo = (x - hi.astype(F32)).astype(BF16)
    return hi, lo


def _split3(x):
    hi = x.astype(BF16)
    r1 = x - hi.astype(F32)
    mid = r1.astype(BF16)
    lo = (r1 - mid.astype(F32)).astype(BF16)
    return hi, mid, lo


def _dot_x3(a, b_hi, b_lo):
    a_hi, a_lo = _split2(a)
    return _dot(a_hi, b_hi) + (_dot(a_hi, b_lo) + _dot(a_lo, b_hi))


def _head_sums(x):
    lo_mask = lax.broadcasted_iota(jnp.int32, (1, LANES), 1) < HEAD
    outs = []
    for p in range(x.shape[1] // LANES):
        xs = x[:, p * LANES:(p + 1) * LANES]
        s_lo = jnp.sum(jnp.where(lo_mask, xs, 0.0), axis=-1, keepdims=True)
        s_hi = jnp.sum(jnp.where(lo_mask, 0.0, xs), axis=-1, keepdims=True)
        outs.append(jnp.where(lo_mask, s_lo, s_hi))
    return outs[0] if len(outs) == 1 else jnp.concatenate(outs, axis=1)


def _sigmoid(x):
    return 1.0 / (1.0 + jnp.exp(-x))


def _silu(x):
    return x * _sigmoid(x)


def _block_diag2(y, lo_mask):
    return jnp.concatenate([jnp.where(lo_mask, y, 0.0), jnp.where(lo_mask, 0.0, y)], axis=0)


def _proj_in_kernel(x_ref, g_ref, w_ref, *out_refs, splits):
    x = x_ref[...]
    ms = jnp.mean(x * x, axis=-1, keepdims=True)
    h = (x * lax.rsqrt(ms + RMS_EPS) * g_ref[...]).astype(BF16)
    for o_ref, (lo, hi) in zip(out_refs, splits):
        o_ref[...] = _dot(h, w_ref[:, lo:hi])


def _proj_in(x2, g, w_bf, widths):
    n, d = x2.shape
    offs = np.concatenate([[0], np.cumsum(widths)])
    splits = tuple((int(offs[i]), int(offs[i + 1])) for i in range(len(widths)))
    return pl.pallas_call(
        functools.partial(_proj_in_kernel, splits=splits),
        grid=(n // ROW_TILE,),
        in_specs=[
            pl.BlockSpec((ROW_TILE, d), lambda i: (i, 0)),
            pl.BlockSpec((1, d), lambda i: (0, 0)),
            pl.BlockSpec(w_bf.shape, lambda i: (0, 0), pipeline_mode=pl.Buffered(1)),
        ],
        out_specs=[pl.BlockSpec((ROW_TILE, w), lambda i: (i, 0)) for w in widths],
        out_shape=[jax.ShapeDtypeStruct((n, w), F32) for w in widths],
        compiler_params=pltpu.CompilerParams(
            dimension_semantics=("arbitrary",), vmem_limit_bytes=VMEM_LIMIT),
        name="proj_in",
    )(x2, g, w_bf)


def _mem_kv_kernel(mem_ref, g_ref, w_ref, kg_ref, k_ref, v_ref, *, xw, hd):
    m = mem_ref[...]
    ms = jnp.mean(m * m, axis=-1, keepdims=True)
    h = (m * lax.rsqrt(ms + RMS_EPS) * g_ref[...]).astype(BF16)
    kv = _dot(h, w_ref[...])
    v_ref[...] = kv[:, xw:]
    for j in range(xw // hd):
        kj = kv[:, j * hd:(j + 1) * hd]
        msk = jnp.mean(kj * kj, axis=-1, keepdims=True)
        k_ref[:, j * hd:(j + 1) * hd] = kj * lax.rsqrt(msk + RMS_EPS) * kg_ref[...]


def _mem_kv(mem2, g, w_bf, kg, m_len, xw, hd):
    n, d = mem2.shape
    return pl.pallas_call(
        functools.partial(_mem_kv_kernel, xw=xw, hd=hd),
        grid=(n // m_len,),
        in_specs=[
            pl.BlockSpec((m_len, d), lambda i: (i, 0)),
            pl.BlockSpec((1, d), lambda i: (0, 0)),
            pl.BlockSpec(w_bf.shape, lambda i: (0, 0)),
            pl.BlockSpec((1, hd), lambda i: (0, 0)),
        ],
        out_specs=[pl.BlockSpec((m_len, xw), lambda i: (i, 0))] * 2,
        out_shape=[jax.ShapeDtypeStruct((n, xw), F32)] * 2,
        compiler_params=pltpu.CompilerParams(
            dimension_semantics=("arbitrary",), vmem_limit_bytes=VMEM_LIMIT),
        name="mem_kv",
    )(mem2, g, w_bf, kg)


def _rwkv_kernel(pa_ref, mu_ref, w0_ref, a0_ref, lora_hi_ref, lora_lo_ref, kk_ref, ka_ref, rk_ref,
                 lng_ref, lnb_ref, ya_ref, carry_ref, s_ref, *, width, lora2):
    nb, tt, _ = pa_ref.shape
    n_pairs = width // LANES
    mixw = 3 * width + lora2
    nrow = nb * CHUNK

    @pl.when(pl.program_id(0) == 0)
    def _():
        carry_ref[...] = jnp.zeros_like(carry_ref)
        s_ref[...] = jnp.zeros_like(s_ref)

    ti = lax.broadcasted_iota(jnp.int32, (CHUNK, LANES), 0)
    li = lax.broadcasted_iota(jnp.int32, (CHUNK, LANES), 1)
    lo_mask = li < HEAD
    lj = jnp.where(lo_mask, li, li - HEAD)
    strict = ti > lj
    incl = ti >= lj
    eye = (ti == lj).astype(F32)
    ri = lax.broadcasted_iota(jnp.int32, (nrow, nrow), 0)
    ci = lax.broadcasted_iota(jnp.int32, (nrow, nrow), 1)
    tri = ((ri >= ci) & (jnp.bitwise_xor(ri, ci) < CHUNK)).astype(BF16)
    row_id = lax.broadcasted_iota(jnp.int32, (nrow, 1), 0)
    mu = mu_ref[...]
    lane_l = lax.broadcasted_iota(jnp.int32, (1, lora2), 1)

    def bd(y):
        return _block_diag2(y, lo_mask).astype(BF16)

    def prep(c):
        r0 = c * CHUNK
        u = jnp.concatenate([pa_ref[b, r0:r0 + CHUNK, 0:mixw] for b in range(nb)], axis=0)
        prev = pltpu.roll(u, 1, 0)
        for b in range(nb):
            before = carry_ref[b:b + 1, :] if c == 0 else pa_ref[b, r0 - 1:r0, 0:mixw]
            prev = jnp.where(row_id == b * CHUNK, before, prev)
        mixed = u + (prev - u) * mu
        r = mixed[:, 0:width]
        k = mixed[:, width:2 * width]
        v = mixed[:, 2 * width:3 * width]
        lo_ra = mixed[:, 3 * width:mixw]
        lo_in = jnp.where(lane_l < lora2 // 2, jnp.tanh(lo_ra), lo_ra)
        proj = _dot_x3(lo_in, lora_hi_ref[...], lora_lo_ref[...])
        lw = -DECAY_SCALE * _sigmoid(w0_ref[...] + proj[:, 0:width])
        a_sig = _sigmoid(a0_ref[...] + proj[:, width:2 * width])
        kk = k * kk_ref[...]
        kkn = kk * lax.rsqrt(jnp.maximum(_head_sums(kk * kk), L2_EPS * L2_EPS))
        k2 = k * (1.0 + (a_sig - 1.0) * ka_ref[...])
        bvec = kkn * a_sig
        l_hi, l_mid, l_lo = _split3(lw)
        cs = _dot(tri, l_hi) + (_dot(tri, l_mid) + _dot(tri, l_lo))
        cs_last = [cs[(b + 1) * CHUNK - 1:(b + 1) * CHUNK, :] for b in range(nb)]
        cs_end = jnp.concatenate([jnp.broadcast_to(cl, (CHUNK, width)) for cl in cs_last], axis=0)
        w_last = [jnp.exp(cl) for cl in cs_last]
        e_in = jnp.exp(cs)
        e_neg = jnp.exp(-cs)
        e_tail = jnp.exp(cs_end - cs)
        return dict(r=r, k2=k2, v=v, w_last=w_last, rt=r * e_in, at=-kkn * jnp.exp(cs - lw),
                    kt=k2 * e_neg, bt=bvec * e_neg, kh=k2 * e_tail, bh=bvec * e_tail)

    n_ch = tt // CHUNK
    pre = [prep(c) for c in range(n_ch)]
    for b in range(nb):
        carry_ref[b:b + 1, :] = pa_ref[b, tt - 1:tt, 0:mixw]

    def blk(c, name, b, p):
        return pre[c][name][b * CHUNK:(b + 1) * CHUNK, p * LANES:(p + 1) * LANES]

    probs = [(c, b, p) for c in range(n_ch) for b in range(nb) for p in range(n_pairs)]
    idx = range(len(probs))
    lhs2 = [jnp.concatenate([blk(c, "at", b, p), blk(c, "rt", b, p)], axis=0).astype(BF16) for c, b, p in probs]
    g = [_dot_nt(lhs2[i], jnp.concatenate([bd(blk(c, "bt", b, p)), bd(blk(c, "kt", b, p))], axis=0))
         for i, (c, b, p) in enumerate(probs)]
    a_ab = [jnp.where(strict, g[i][0:CHUNK, 0:LANES], 0.0) for i in idx]
    a_ak = [jnp.where(strict, g[i][0:CHUNK, LANES:], 0.0) for i in idx]
    a_r = [jnp.concatenate([jnp.where(incl, g[i][CHUNK:, 0:LANES], 0.0),
                            jnp.where(incl, g[i][CHUNK:, LANES:], 0.0)], axis=1).astype(BF16) for i in idx]
    v_bd = [bd(blk(c, "v", b, p)) for c, b, p in probs]
    av = [_dot(a_ak[i].astype(BF16), v_bd[i]) for i in idx]
    x = [eye + a_ab[i] for i in idx]
    pw = [_dot(a_ab[i].astype(BF16), bd(a_ab[i])) for i in idx]
    for _ in range(4):
        px = [_dot(jnp.concatenate([pw[i], x[i]], axis=0).astype(BF16), bd(pw[i])) for i in idx]
        pw = [px[i][0:CHUNK] for i in idx]
        x = [x[i] + px[i][CHUNK:] for i in idx]
    x = [(x[i] + _dot(x[i].astype(BF16), bd(pw[i]))).astype(BF16) for i in idx]

    n_bp = nb * n_pairs
    s_cur = [s_ref[j] for j in range(n_bp)]
    inv_n = 1.0 / HEAD
    for c in range(n_ch):
        r0 = c * CHUNK
        ii = [c * n_bp + j for j in range(n_bp)]
        bp = [probs[i][1:] for i in ii]
        asrs = [_dot_nt(lhs2[i], bd(s_cur[j])) for j, i in enumerate(ii)]
        z = [asrs[j][0:CHUNK] + av[i] for j, i in enumerate(ii)]
        u_p = [_dot(x[i], bd(z[j])) for j, i in enumerate(ii)]
        d = [_dot_tn(jnp.concatenate([u_p[j], blk(c, "v", b, p)], axis=0).astype(BF16),
                     jnp.concatenate([blk(c, "bh", b, p), blk(c, "kh", b, p)], axis=0).astype(BF16))
             for j, (b, p) in enumerate(bp)]
        s_cur = [s_cur[j] * pre[c]["w_last"][b][:, p * LANES:(p + 1) * LANES]
                 + jnp.where(lo_mask, d[j][0:HEAD], d[j][HEAD:]) for j, (b, p) in enumerate(bp)]
        y = [asrs[j][CHUNK:] + _dot(a_r[i], jnp.concatenate([bd(u_p[j]), v_bd[i]], axis=0))
             for j, i in enumerate(ii)]
        y_all = jnp.concatenate(
            [jnp.concatenate([y[b * n_pairs + p] for p in range(n_pairs)], axis=1) for b in range(nb)], axis=0)
        yc = y_all - _head_sums(y_all) * inv_n
        var = _head_sums(yc * yc) * inv_n
        yn = yc * lax.rsqrt(var + LNX_EPS) * lng_ref[...] + lnb_ref[...]
        bonus = _head_sums(pre[c]["r"] * pre[c]["k2"] * rk_ref[...]) * pre[c]["v"]
        for b in range(nb):
            bs = slice(b * CHUNK, (b + 1) * CHUNK)
            ya_ref[b, r0:r0 + CHUNK, :] = ((yn[bs] + bonus[bs])
                                           * _silu(pa_ref[b, r0:r0 + CHUNK, mixw:mixw + width]))
    for j in range(n_bp):
        s_ref[j] = s_cur[j]


def _rwkv(pa, bsz, t, width, lora2, params):
    pa3 = pa.reshape(bsz, t, pa.shape[1])
    full = lambda a: pl.BlockSpec(a.shape, lambda i: (0,) * a.ndim)
    ya = pl.pallas_call(
        functools.partial(_rwkv_kernel, width=width, lora2=lora2),
        grid=(t // ROW_TILE,),
        in_specs=[pl.BlockSpec((bsz, ROW_TILE, pa.shape[1]), lambda i: (0, i, 0))]
        + [full(a) for a in params],
        out_specs=pl.BlockSpec((bsz, ROW_TILE, width), lambda i: (0, i, 0)),
        out_shape=jax.ShapeDtypeStruct((bsz, t, width), F32),
        scratch_shapes=[
            pltpu.VMEM((bsz, 3 * width + lora2), F32),
            pltpu.VMEM((bsz * (width // LANES), HEAD, LANES), F32),
        ],
        compiler_params=pltpu.CompilerParams(
            dimension_semantics=("arbitrary",), vmem_limit_bytes=VMEM_LIMIT),
        name="rwkv",
    )(pa3, *params)
    return ya.reshape(bsz * t, width)


def _attn_out_kernel(x_ref, pb_ref, pc_ref, pg_ref, ya_ref, pos_ref, km_ref, vm_ref,
                     inv_ref, qg_ref, kg_ref, sink_ref, xqg_ref,
                     wa_ref, wb_ref, wc_ref, wo_ref, o_ref, kprev_ref, vprev_ref,
                     *, sw_w, kv_w, x_w, x_hd):
    tm = x_ref.shape[0]
    d = x_ref.shape[1]
    first = pl.program_id(1) == 0

    @pl.when(first)
    def _():
        kprev_ref[...] = jnp.zeros_like(kprev_ref)
        vprev_ref[...] = jnp.zeros_like(vprev_ref)


    half = HEAD // 2
    n_grp = LANES // half
    qrows = tm // n_grp
    pos = pos_ref[...].astype(F32)
    lane_grp = lax.broadcasted_iota(jnp.int32, (1, LANES), 1) // half
    pos_d = pos[0:qrows]
    for j in range(1, n_grp):
        pos_d = jnp.where(lane_grp == j, pos[j * qrows:(j + 1) * qrows], pos_d)
    ang_d = pos_d * inv_ref[...]
    cos_d = jnp.cos(ang_d)
    sin_d = jnp.sin(ang_d)

    def spread(tab):
        rolled = [tab] + [pltpu.roll(tab, half * k, 1) for k in range(1, n_grp)]
        quarters = []
        for j in range(n_grp):
            cj = rolled[(0 - j) % n_grp]
            for gl in range(1, n_grp):
                cj = jnp.where(lane_grp == gl, rolled[(gl - j) % n_grp], cj)
            quarters.append(cj)
        return jnp.concatenate(quarters, axis=0)

    cos_p = spread(cos_d)
    sin_p = spread(sin_d)

    def rope(xn):
        w = xn.shape[1]
        reps = w // LANES
        first_half = (lax.broadcasted_iota(jnp.int32, (1, w), 1) & (HEAD - 1)) < half
        cos_t = jnp.concatenate([cos_p] * reps, axis=1) if reps > 1 else cos_p
        sin_t = jnp.concatenate([sin_p] * reps, axis=1) if reps > 1 else sin_p
        rot = jnp.where(first_half, -pltpu.roll(xn, w - half, 1), pltpu.roll(xn, half, 1))
        return xn * cos_t + rot * sin_t

    q = pb_ref[:, 0:sw_w]
    qn = q * lax.rsqrt(_head_sums(q * q) * (1.0 / HEAD) + RMS_EPS) * (qg_ref[...] * (HEAD ** -0.5))
    qr = rope(qn).astype(BF16)
    kx = pb_ref[:, sw_w:sw_w + kv_w]
    kn = kx * lax.rsqrt(_head_sums(kx * kx) * (1.0 / HEAD) + RMS_EPS) * kg_ref[...]
    kr = rope(kn)
    vx = pb_ref[:, sw_w + kv_w:sw_w + 2 * kv_w]
    n_xh = x_w // x_hd
    xqn = []
    for j in range(n_xh):
        xq = pc_ref[:, j * x_hd:(j + 1) * x_hd]
        xqn.append((xq * lax.rsqrt(jnp.mean(xq * xq, axis=-1, keepdims=True) + RMS_EPS)
                    * (xqg_ref[...] * (x_hd ** -0.5))).astype(BF16))

    lo_mask = lax.broadcasted_iota(jnp.int32, (1, LANES), 1) < HEAD
    qi = lax.broadcasted_iota(jnp.int32, (SW_BLOCK, 2 * SW_BLOCK), 0)
    kj = lax.broadcasted_iota(jnp.int32, (SW_BLOCK, 2 * SW_BLOCK), 1)
    allowed = (kj > qi) & (kj <= qi + SW_BLOCK)
    n_sub = tm // SW_BLOCK
    n_kv = kv_w // HEAD
    n_pairs = sw_w // LANES
    pairs_per_kv = n_pairs // n_kv

    s_c = [_dot_nt(xqn[j], km_ref[:, j * x_hd:(j + 1) * x_hd].astype(BF16)) for j in range(n_xh)]
    v_bds, masks, s_b = [], [], []
    for sb in range(n_sub):
        rs = slice(sb * SW_BLOCK, (sb + 1) * SW_BLOCK)
        if sb == 0:
            kband = jnp.concatenate([kprev_ref[...], kr[rs]], axis=0)
            vband = jnp.concatenate([vprev_ref[...], vx[rs]], axis=0)
            masks.append(allowed & (kj >= jnp.where(first, SW_BLOCK, 0)))
        else:
            kband = kr[(sb - 1) * SW_BLOCK:(sb + 1) * SW_BLOCK]
            vband = vx[(sb - 1) * SW_BLOCK:(sb + 1) * SW_BLOCK]
            masks.append(allowed)
        kband_sw = pltpu.roll(kband, HEAD, 1)
        vband_sw = pltpu.roll(vband, HEAD, 1)
        for g in range(n_kv):
            k_lo = kband if g == 0 else kband_sw
            k_hi = kband_sw if g == 0 else kband
            v_lo = vband if g == 0 else vband_sw
            v_hi = vband_sw if g == 0 else vband
            k_bd = jnp.concatenate([jnp.where(lo_mask, k_lo, 0.0), jnp.where(lo_mask, 0.0, k_hi)],
                                   axis=0).astype(BF16)
            v_bds.append(jnp.concatenate([jnp.where(lo_mask, v_lo, 0.0), jnp.where(lo_mask, 0.0, v_hi)],
                                         axis=0).astype(BF16))
            for pp in range(pairs_per_kv):
                p = g * pairs_per_kv + pp
                s_b.append(_dot_nt(qr[rs, p * LANES:(p + 1) * LANES], k_bd))
    kprev_ref[...] = kr[tm - SW_BLOCK:, :]
    vprev_ref[...] = vx[tm - SW_BLOCK:, :]

    proj_a = _dot(ya_ref[...].astype(BF16), wa_ref[...])

    p_c = []
    for j in range(n_xh):
        m = jnp.max(s_c[j], axis=-1, keepdims=True)
        e = jnp.exp(s_c[j] - m)
        p_c.append((e / jnp.sum(e, axis=-1, keepdims=True)).astype(BF16))
    p_b = []
    for sb in range(n_sub):
        for p in range(n_pairs):
            s2 = s_b[sb * n_pairs + p]
            probs = []
            for hh in range(2):
                s = jnp.where(masks[sb], s2[:, hh * 2 * SW_BLOCK:(hh + 1) * 2 * SW_BLOCK], NEG_INF)
                sink = sink_ref[0, 2 * p + hh]
                m = jnp.maximum(jnp.max(s, axis=-1, keepdims=True), sink)
                e = jnp.exp(s - m)
                den = jnp.sum(e, axis=-1, keepdims=True) + jnp.exp(sink - m)
                probs.append((e / den).astype(BF16))
            p_b.append(jnp.concatenate(probs, axis=1))
    o_c = [_dot(p_c[j], vm_ref[:, j * x_hd:(j + 1) * x_hd].astype(BF16)) for j in range(n_xh)]
    o_b = [[_dot(p_b[sb * n_pairs + p], v_bds[sb * n_kv + p // pairs_per_kv]) for p in range(n_pairs)]
           for sb in range(n_sub)]

    y_c = (jnp.concatenate(o_c, axis=1) * _silu(pc_ref[:, x_w:2 * x_w])).astype(BF16)
    proj_c = _dot(y_c, wc_ref[...])
    y_b = jnp.concatenate([jnp.concatenate(o_b[sb], axis=1) for sb in range(n_sub)], axis=0)
    y_b = (y_b * _silu(pb_ref[:, sw_w + 2 * kv_w:2 * sw_w + 2 * kv_w])).astype(BF16)
    proj_b = _dot(y_b, wb_ref[...])
    merged = _sigmoid(pg_ref[:, 0:d]) * proj_a
    merged = merged + _sigmoid(pg_ref[:, 2 * d:3 * d]) * proj_c
    merged = merged + _sigmoid(pg_ref[:, d:2 * d]) * proj_b
    o_ref[...] = x_ref[...] + _dot(merged.astype(BF16), wo_ref[...])


def _attn_out(x2, pb, pc, pg, ya, pos2, km, vm, params, bsz, t, m_len, sw_w, kv_w, x_w, x_hd):
    n, d = x2.shape
    nt = t // ROW_TILE
    row = lambda a: pl.BlockSpec((ROW_TILE, a.shape[1]), lambda b, i: (b * nt + i, 0))
    full = lambda a: pl.BlockSpec(a.shape, lambda b, i: (0,) * a.ndim)
    memspec = pl.BlockSpec((m_len, x_w), lambda b, i: (b, 0))
    in_specs = [row(x2), row(pb), row(pc), row(pg), row(ya), row(pos2), memspec, memspec]
    for a in params:
        in_specs.append(full(a))
    in_specs[8 + 3] = pl.BlockSpec(memory_space=pltpu.SMEM)
    return pl.pallas_call(
        functools.partial(_attn_out_kernel, sw_w=sw_w, kv_w=kv_w, x_w=x_w, x_hd=x_hd),
        grid=(bsz, nt),
        in_specs=in_specs,
        out_specs=pl.BlockSpec((ROW_TILE, d), lambda b, i: (b * nt + i, 0)),
        out_shape=jax.ShapeDtypeStruct((n, d), F32),
        scratch_shapes=[
            pltpu.VMEM((SW_BLOCK, kv_w), F32),
            pltpu.VMEM((SW_BLOCK, kv_w), F32),
        ],
        compiler_params=pltpu.CompilerParams(
            dimension_semantics=("arbitrary", "arbitrary"), vmem_limit_bytes=VMEM_LIMIT),
        name="attn_out",
    )(x2, pb, pc, pg, ya, pos2, km, vm, *params)


def kernel(x, mem, positions, norm_g, mem_norm_g, w_in, mu_rkv, mu_wa, w0, w2, a0, a2, k_k, k_a, r_k,
           lnx_g, lnx_b, q_norm_g, k_norm_g, sinks, xq_norm_g, xk_norm_g, w_mem_kv,
           w_proj_a, w_proj_b, w_proj_c, w_out):
    bsz, t, d = x.shape
    m_len = mem.shape[1]
    depth = w_in.shape[0]
    rw_w = w0.shape[1]
    lora = w2.shape[1]
    sw_w = w_proj_b.shape[1]
    x_w = w_proj_c.shape[1]
    x_hd = xq_norm_g.shape[1]
    kv_w = (w_in.shape[2] - (4 * rw_w + 2 * lora) - 2 * sw_w - 2 * x_w - 3 * d) // 2
    assert t % ROW_TILE == 0 and rw_w % LANES == 0 and sw_w % LANES == 0 and kv_w == LANES
    assert q_norm_g.shape[1] == HEAD and r_k.shape[2] == HEAD and 2 * lora == LANES
    widths = (4 * rw_w + 2 * lora, 2 * sw_w + 2 * kv_w, 2 * x_w, 3 * d)
    n = bsz * t
    x2 = x.reshape(n, d)
    mem2 = mem.reshape(bsz * m_len, d)
    pos2 = positions.reshape(n, 1)
    half = HEAD // 2
    inv = ROPE_THETA ** (-(jnp.arange(LANES) % half).astype(F32) / half)
    inv = inv.reshape(1, LANES)
    for l in range(depth):
        row = lambda a: a[l].reshape(1, -1)
        pa, pb, pc, pg = _proj_in(x2, row(norm_g), w_in[l].astype(BF16), widths)
        km, vm = _mem_kv(mem2, row(mem_norm_g), w_mem_kv[l].astype(BF16), row(xk_norm_g), m_len, x_w, x_hd)
        mu = jnp.concatenate([mu_rkv[l].reshape(1, -1), mu_wa[l].reshape(1, -1)], axis=1)
        zeros = jnp.zeros((lora, rw_w), F32)
        lora_w = jnp.concatenate([jnp.concatenate([w2[l], zeros], axis=1),
                                  jnp.concatenate([zeros, a2[l]], axis=1)], axis=0)
        lora_hi = lora_w.astype(BF16)
        lora_lo = (lora_w - lora_hi.astype(F32)).astype(BF16)
        rw_params = (mu, row(w0), row(a0), lora_hi, lora_lo, row(k_k), row(k_a), row(r_k),
                     row(lnx_g), row(lnx_b))
        ya = _rwkv(pa, bsz, t, rw_w, 2 * lora, rw_params)
        at_params = (inv, jnp.tile(row(q_norm_g), (1, sw_w // HEAD)), jnp.tile(row(k_norm_g), (1, kv_w // HEAD)),
                     row(sinks), row(xq_norm_g),
                     w_proj_a[l].astype(BF16), w_proj_b[l].astype(BF16), w_proj_c[l].astype(BF16),
                     w_out[l].astype(BF16))
        x2 = _attn_out(x2, pb, pc, pg, ya, pos2, km, vm, at_params, bsz, t, m_len, sw_w, kv_w, x_w, x_hd)
    return x2.reshape(bsz, t, d)
```

```python
import functools
import math

import jax
import jax.numpy as jnp
import numpy as np
from jax import lax
from jax.experimental import pallas as pl
from jax.experimental.pallas import tpu as pltpu

F32 = jnp.float32
BF16 = jnp.bfloat16

RMS_EPS = 1e-6
LNX_EPS = 64e-5
L2_EPS = 1e-12
DECAY_SCALE = math.exp(-0.5)
ROPE_THETA = 10000.0
NEG_INF = -1e30

HEAD = 64
LANES = 128
CHUNK = 64
SW_BLOCK = 128
ROW_TILE = 256
VMEM_LIMIT = 56 * 1024 * 1024


def _dot(a, b):
    return jnp.dot(a, b, preferred_element_type=F32)


def _dot_nt(a, b):
    return lax.dot_general(a, b, (((1,), (1,)), ((), ())), preferred_element_type=F32)


def _dot_tn(a, b):
    return lax.dot_general(a, b, (((0,), (0,)), ((), ())), preferred_element_type=F32)


def _split2(x):
    hi = x.astype(BF16)
    lo = (x - hi.astype(F32)).astype(BF16)
    return hi, lo


def _split3(x):
    hi = x.astype(BF16)
    r1 = x - hi.astype(F32)
    mid = r1.astype(BF16)
    lo = (r1 - mid.astype(F32)).astype(BF16)
    return hi, mid, lo


def _dot_x3(a, b_hi, b_lo):
    a_hi, a_lo = _split2(a)
    return _dot(a_hi, b_hi) + (_dot(a_hi, b_lo) + _dot(a_lo, b_hi))


def _head_sums(x):
    lo_mask = lax.broadcasted_iota(jnp.int32, (1, LANES), 1) < HEAD
    outs = []
    for p in range(x.shape[1] // LANES):
        xs = x[:, p * LANES:(p + 1) * LANES]
        s_lo = jnp.sum(jnp.where(lo_mask, xs, 0.0), axis=-1, keepdims=True)
        s_hi = jnp.sum(jnp.where(lo_mask, 0.0, xs), axis=-1, keepdims=True)
        outs.append(jnp.where(lo_mask, s_lo, s_hi))
    return outs[0] if len(outs) == 1 else jnp.concatenate(outs, axis=1)


def _sigmoid(x):
    return 1.0 / (1.0 + jnp.exp(-x))


def _silu(x):
    return x * _sigmoid(x)


def _block_diag2(y, lo_mask):
    return jnp.concatenate([jnp.where(lo_mask, y, 0.0), jnp.where(lo_mask, 0.0, y)], axis=0)


def _proj_in_kernel(x_ref, g_ref, w_ref, *out_refs, splits):
    x = x_ref[...]
    ms = jnp.mean(x * x, axis=-1, keepdims=True)
    h = (x * lax.rsqrt(ms + RMS_EPS) * g_ref[...]).astype(BF16)
    for o_ref, (lo, hi) in zip(out_refs, splits):
        o_ref[...] = _dot(h, w_ref[:, lo:hi])


def _proj_in(x2, g, w_bf, widths):
    n, d = x2.shape
    offs = np.concatenate([[0], np.cumsum(widths)])
    splits = tuple((int(offs[i]), int(offs[i + 1])) for i in range(len(widths)))
    return pl.pallas_call(
        functools.partial(_proj_in_kernel, splits=splits),
        grid=(n // ROW_TILE,),
        in_specs=[
            pl.BlockSpec((ROW_TILE, d), lambda i: (i, 0)),
            pl.BlockSpec((1, d), lambda i: (0, 0)),
            pl.BlockSpec(w_bf.shape, lambda i: (0, 0), pipeline_mode=pl.Buffered(1)),
        ],
        out_specs=[pl.BlockSpec((ROW_TILE, w), lambda i: (i, 0)) for w in widths],
        out_shape=[jax.ShapeDtypeStruct((n, w), F32) for w in widths],
        compiler_params=pltpu.CompilerParams(
            dimension_semantics=("arbitrary",), vmem_limit_bytes=VMEM_LIMIT),
        name="proj_in",
    )(x2, g, w_bf)


def _mem_kv_kernel(mem_ref, g_ref, w_ref, kg_ref, k_ref, v_ref, *, xw, hd):
    m = mem_ref[...]
    ms = jnp.mean(m * m, axis=-1, keepdims=True)
    h = (m * lax.rsqrt(ms + RMS_EPS) * g_ref[...]).astype(BF16)
    kv = _dot(h, w_ref[...])
    v_ref[...] = kv[:, xw:]
    for j in range(xw // hd):
        kj = kv[:, j * hd:(j + 1) * hd]
        msk = jnp.mean(kj * kj, axis=-1, keepdims=True)
        k_ref[:, j * hd:(j + 1) * hd] = kj * lax.rsqrt(msk + RMS_EPS) * kg_ref[...]


def _mem_kv(mem2, g, w_bf, kg, m_len, xw, hd):
    n, d = mem2.shape
    return pl.pallas_call(
        functools.partial(_mem_kv_kernel, xw=xw, hd=hd),
        grid=(n // m_len,),
        in_specs=[
            pl.BlockSpec((m_len, d), lambda i: (i, 0)),
            pl.BlockSpec((1, d), lambda i: (0, 0)),
            pl.BlockSpec(w_bf.shape, lambda i: (0, 0)),
            pl.BlockSpec((1, hd), lambda i: (0, 0)),
        ],
        out_specs=[pl.BlockSpec((m_len, xw), lambda i: (i, 0))] * 2,
        out_shape=[jax.ShapeDtypeStruct((n, xw), F32)] * 2,
        compiler_params=pltpu.CompilerParams(
            dimension_semantics=("arbitrary",), vmem_limit_bytes=VMEM_LIMIT),
        name="mem_kv",
    )(mem2, g, w_bf, kg)


def _rwkv_kernel(pa_ref, mu_ref, w0_ref, a0_ref, lora_hi_ref, lora_lo_ref, kk_ref, ka_ref, rk_ref,
                 lng_ref, lnb_ref, ya_ref, carry_ref, s_ref, *, width, lora2):
    nb, tt, _ = pa_ref.shape
    n_pairs = width // LANES
    mixw = 3 * width + lora2
    nrow = nb * CHUNK

    @pl.when(pl.program_id(0) == 0)
    def _():
        carry_ref[...] = jnp.zeros_like(carry_ref)
        s_ref[...] = jnp.zeros_like(s_ref)

    ti = lax.broadcasted_iota(jnp.int32, (CHUNK, LANES), 0)
    li = lax.broadcasted_iota(jnp.int32, (CHUNK, LANES), 1)
    lo_mask = li < HEAD
    lj = jnp.where(lo_mask, li, li - HEAD)
    strict = ti > lj
    incl = ti >= lj
    eye = (ti == lj).astype(F32)
    ri = lax.broadcasted_iota(jnp.int32, (nrow, nrow), 0)
    ci = lax.broadcasted_iota(jnp.int32, (nrow, nrow), 1)
    tri = ((ri >= ci) & (jnp.bitwise_xor(ri, ci) < CHUNK)).astype(BF16)
    row_id = lax.broadcasted_iota(jnp.int32, (nrow, 1), 0)
    mu = mu_ref[...]
    lane_l = lax.broadcasted_iota(jnp.int32, (1, lora2), 1)

    def bd(y):
        return _block_diag2(y, lo_mask).astype(BF16)

    def prep(c):
        r0 = c * CHUNK
        u = jnp.concatenate([pa_ref[b, r0:r0 + CHUNK, 0:mixw] for b in range(nb)], axis=0)
        prev = pltpu.roll(u, 1, 0)
        for b in range(nb):
            before = carry_ref[b:b + 1, :] if c == 0 else pa_ref[b, r0 - 1:r0, 0:mixw]
            prev = jnp.where(row_id == b * CHUNK, before, prev)
        mixed = u + (prev - u) * mu
        r = mixed[:, 0:width]
        k = mixed[:, width:2 * width]
        v = mixed[:, 2 * width:3 * width]
        lo_ra = mixed[:, 3 * width:mixw]
        lo_in = jnp.where(lane_l < lora2 // 2, jnp.tanh(lo_ra), lo_ra)
        proj = _dot_x3(lo_in, lora_hi_ref[...], lora_lo_ref[...])
        lw = -DECAY_SCALE * _sigmoid(w0_ref[...] + proj[:, 0:width])
        a_sig = _sigmoid(a0_ref[...] + proj[:, width:2 * width])
        kk = k * kk_ref[...]
        kkn = kk * lax.rsqrt(jnp.maximum(_head_sums(kk * kk), L2_EPS * L2_EPS))
        k2 = k * (1.0 + (a_sig - 1.0) * ka_ref[...])
        bvec = kkn * a_sig
        l_hi, l_mid, l_lo = _split3(lw)
        cs = _dot(tri, l_hi) + (_dot(tri, l_mid) + _dot(tri, l_lo))
        cs_last = [cs[(b + 1) * CHUNK - 1:(b + 1) * CHUNK, :] for b in range(nb)]
        cs_end = jnp.concatenate([jnp.broadcast_to(cl, (CHUNK, width)) for cl in cs_last], axis=0)
        w_last = [jnp.exp(cl) for cl in cs_last]
        e_in = jnp.exp(cs)
        e_neg = jnp.exp(-cs)
        e_tail = jnp.exp(cs_end - cs)
        return dict(r=r, k2=k2, v=v, w_last=w_last, rt=r * e_in, at=-kkn * jnp.exp(cs - lw),
                    kt=k2 * e_neg, bt=bvec * e_neg, kh=k2 * e_tail, bh=bvec * e_tail)

    n_ch = tt // CHUNK
    pre = [prep(c) for c in range(n_ch)]
    for b in range(nb):
        carry_ref[b:b + 1, :] = pa_ref[b, tt - 1:tt, 0:mixw]

    def blk(c, name, b, p):
        return pre[c][name][b * CHUNK:(b + 1) * CHUNK, p * LANES:(p + 1) * LANES]

    probs = [(c, b, p) for c in range(n_ch) for b in range(nb) for p in range(n_pairs)]
    idx = range(len(probs))
    lhs2 = [jnp.concatenate([blk(c, "at", b, p), blk(c, "rt", b, p)], axis=0).astype(BF16) for c, b, p in probs]
    g = [_dot_nt(lhs2[i], jnp.concatenate([bd(blk(c, "bt", b, p)), bd(blk(c, "kt", b, p))], axis=0))
         for i, (c, b, p) in enumerate(probs)]
    a_ab = [jnp.where(strict, g[i][0:CHUNK, 0:LANES], 0.0) for i in idx]
    a_ak = [jnp.where(strict, g[i][0:CHUNK, LANES:], 0.0) for i in idx]
    a_r = [jnp.concatenate([jnp.where(incl, g[i][CHUNK:, 0:LANES], 0.0),
                            jnp.where(incl, g[i][CHUNK:, LANES:], 0.0)], axis=1).astype(BF16) for i in idx]
    v_bd = [bd(blk(c, "v", b, p)) for c, b, p in probs]
    av = [_dot(a_ak[i].astype(BF16), v_bd[i]) for i in idx]
    x = [eye + a_ab[i] for i in idx]
    pw = [_dot(a_ab[i].astype(BF16), bd(a_ab[i])) for i in idx]
    for _ in range(4):
        px = [_dot(jnp.concatenate([pw[i], x[i]], axis=0).astype(BF16), bd(pw[i])) for i in idx]
        pw = [px[i][0:CHUNK] for i in idx]
        x = [x[i] + px[i][CHUNK:] for i in idx]
    x = [(x[i] + _dot(x[i].astype(BF16), bd(pw[i]))).astype(BF16) for i in idx]

    n_bp = nb * n_pairs
    s_cur = [s_ref[j] for j in range(n_bp)]
    inv_n = 1.0 / HEAD
    for c in range(n_ch):
        r0 = c * CHUNK
        ii = [c * n_bp + j for j in range(n_bp)]
        bp = [probs[i][1:] for i in ii]
        asrs = [_dot_nt(lhs2[i], bd(s_cur[j])) for j, i in enumerate(ii)]
        z = [asrs[j][0:CHUNK] + av[i] for j, i in enumerate(ii)]
        u_p = [_dot(x[i], bd(z[j])) for j, i in enumerate(ii)]
        d = [_dot_tn(jnp.concatenate([u_p[j], blk(c, "v", b, p)], axis=0).astype(BF16),
                     jnp.concatenate([blk(c, "bh", b, p), blk(c, "kh", b, p)], axis=0).astype(BF16))
             for j, (b, p) in enumerate(bp)]
        s_cur = [s_cur[j] * pre[c]["w_last"][b][:, p * LANES:(p + 1) * LANES]
                 + jnp.where(lo_mask, d[j][0:HEAD], d[j][HEAD:]) for j, (b, p) in enumerate(bp)]
        y = [asrs[j][CHUNK:] + _dot(a_r[i], jnp.concatenate([bd(u_p[j]), v_bd[i]], axis=0))
             for j, i in enumerate(ii)]
        y_all = jnp.concatenate(
            [jnp.concatenate([y[b * n_pairs + p] for p in range(n_pairs)], axis=1) for b in range(nb)], axis=0)
        yc = y_all - _head_sums(y_all) * inv_n
        var = _head_sums(yc * yc) * inv_n
        yn = yc * lax.rsqrt(var + LNX_EPS) * lng_ref[...] + lnb_ref[...]
        bonus = _head_sums(pre[c]["r"] * pre[c]["k2"] * rk_ref[...]) * pre[c]["v"]
        for b in range(nb):
            bs = slice(b * CHUNK, (b + 1) * CHUNK)
            ya_ref[b, r0:r0 + CHUNK, :] = ((yn[bs] + bonus[bs])
                                           * _silu(pa_ref[b, r0:r0 + CHUNK, mixw:mixw + width]))
    for j in range(n_bp):
        s_ref[j] = s_cur[j]


def _rwkv(pa, bsz, t, width, lora2, params):
    pa3 = pa.reshape(bsz, t, pa.shape[1])
    full = lambda a: pl.BlockSpec(a.shape, lambda i: (0,) * a.ndim)
    ya = pl.pallas_call(
        functools.partial(_rwkv_kernel, width=width, lora2=lora2),
        grid=(t // ROW_TILE,),
        in_specs=[pl.BlockSpec((bsz, ROW_TILE, pa.shape[1]), lambda i: (0, i, 0))]
        + [full(a) for a in params],
        out_specs=pl.BlockSpec((bsz, ROW_TILE, width), lambda i: (0, i, 0)),
        out_shape=jax.ShapeDtypeStruct((bsz, t, width), F32),
        scratch_shapes=[
            pltpu.VMEM((bsz, 3 * width + lora2), F32),
            pltpu.VMEM((bsz * (width // LANES), HEAD, LANES), F32),
        ],
        compiler_params=pltpu.CompilerParams(
            dimension_semantics=("arbitrary",), vmem_limit_bytes=VMEM_LIMIT),
        name="rwkv",
    )(pa3, *params)
    return ya.reshape(bsz * t, width)


def _attn_out_kernel(x_ref, ya_ref, pos_ref, km_ref, vm_ref, ng_ref, win_ref,
                     inv_ref, qg_ref, kg_ref, sink_ref, xqg_ref,
                     wa_ref, wb_ref, wc_ref, wo_ref, o_ref, kprev_ref, vprev_ref,
                     *, sw_w, kv_w, x_w, x_hd):
    tm = x_ref.shape[0]
    d = x_ref.shape[1]
    first = pl.program_id(1) == 0

    @pl.when(first)
    def _():
        kprev_ref[...] = jnp.zeros_like(kprev_ref)
        vprev_ref[...] = jnp.zeros_like(vprev_ref)


    xin = x_ref[...]
    h = (xin * lax.rsqrt(jnp.mean(xin * xin, axis=-1, keepdims=True) + RMS_EPS) * ng_ref[...]).astype(BF16)
    b_w = 2 * sw_w + 2 * kv_w
    c_w = 2 * x_w
    pc = _dot(h, win_ref[:, b_w:b_w + c_w])
    pb = _dot(h, win_ref[:, 0:b_w])

    half = HEAD // 2
    n_grp = LANES // half
    qrows = tm // n_grp
    pos = pos_ref[...].astype(F32)
    lane_grp = lax.broadcasted_iota(jnp.int32, (1, LANES), 1) // half
    pos_d = pos[0:qrows]
    for j in range(1, n_grp):
        pos_d = jnp.where(lane_grp == j, pos[j * qrows:(j + 1) * qrows], pos_d)
    ang_d = pos_d * inv_ref[...]
    cos_d = jnp.cos(ang_d)
    sin_d = jnp.sin(ang_d)

    def spread(tab):
        rolled = [tab] + [pltpu.roll(tab, half * k, 1) for k in range(1, n_grp)]
        quarters = []
        for j in range(n_grp):
            cj = rolled[(0 - j) % n_grp]
            for gl in range(1, n_grp):
                cj = jnp.where(lane_grp == gl, rolled[(gl - j) % n_grp], cj)
            quarters.append(cj)
        return jnp.concatenate(quarters, axis=0)

    cos_p = spread(cos_d)
    sin_p = spread(sin_d)

    def rope(xn):
        w = xn.shape[1]
        reps = w // LANES
        first_half = (lax.broadcasted_iota(jnp.int32, (1, w), 1) & (HEAD - 1)) < half
        cos_t = jnp.concatenate([cos_p] * reps, axis=1) if reps > 1 else cos_p
        sin_t = jnp.concatenate([sin_p] * reps, axis=1) if reps > 1 else sin_p
        rot = jnp.where(first_half, -pltpu.roll(xn, w - half, 1), pltpu.roll(xn, half, 1))
        return xn * cos_t + rot * sin_t

    q = pb[:,0:sw_w]
    qn = q * lax.rsqrt(_head_sums(q * q) * (1.0 / HEAD) + RMS_EPS) * (qg_ref[...] * (HEAD ** -0.5))
    qr = rope(qn).astype(BF16)
    kx = pb[:,sw_w:sw_w + kv_w]
    kn = kx * lax.rsqrt(_head_sums(kx * kx) * (1.0 / HEAD) + RMS_EPS) * kg_ref[...]
    kr = rope(kn)
    vx = pb[:,sw_w + kv_w:sw_w + 2 * kv_w]
    n_xh = x_w // x_hd
    xqn = []
    for j in range(n_xh):
        xq = pc[:,j * x_hd:(j + 1) * x_hd]
        xqn.append((xq * lax.rsqrt(jnp.mean(xq * xq, axis=-1, keepdims=True) + RMS_EPS)
                    * (xqg_ref[...] * (x_hd ** -0.5))).astype(BF16))

    lo_mask = lax.broadcasted_iota(jnp.int32, (1, LANES), 1) < HEAD
    qi = lax.broadcasted_iota(jnp.int32, (SW_BLOCK, 2 * SW_BLOCK), 0)
    kj = lax.broadcasted_iota(jnp.int32, (SW_BLOCK, 2 * SW_BLOCK), 1)
    allowed = (kj > qi) & (kj <= qi + SW_BLOCK)
    n_sub = tm // SW_BLOCK
    n_kv = kv_w // HEAD
    n_pairs = sw_w // LANES
    pairs_per_kv = n_pairs // n_kv

    s_c = [_dot_nt(xqn[j], km_ref[:, j * x_hd:(j + 1) * x_hd].astype(BF16)) for j in range(n_xh)]
    v_bds, masks, s_b = [], [], []
    for sb in range(n_sub):
        rs = slice(sb * SW_BLOCK, (sb + 1) * SW_BLOCK)
        if sb == 0:
            kband = jnp.concatenate([kprev_ref[...], kr[rs]], axis=0)
            vband = jnp.concatenate([vprev_ref[...], vx[rs]], axis=0)
            masks.append(allowed & (kj >= jnp.where(first, SW_BLOCK, 0)))
        else:
            kband = kr[(sb - 1) * SW_BLOCK:(sb + 1) * SW_BLOCK]
            vband = vx[(sb - 1) * SW_BLOCK:(sb + 1) * SW_BLOCK]
            masks.append(allowed)
        kband_sw = pltpu.roll(kband, HEAD, 1)
        vband_sw = pltpu.roll(vband, HEAD, 1)
        for g in range(n_kv):
            k_lo = kband if g == 0 else kband_sw
            k_hi = kband_sw if g == 0 else kband
            v_lo = vband if g == 0 else vband_sw
            v_hi = vband_sw if g == 0 else vband
            k_bd = jnp.concatenate([jnp.where(lo_mask, k_lo, 0.0), jnp.where(lo_mask, 0.0, k_hi)],
                                   axis=0).astype(BF16)
            v_bds.append(jnp.concatenate([jnp.where(lo_mask, v_lo, 0.0), jnp.where(lo_mask, 0.0, v_hi)],
                                         axis=0).astype(BF16))
            for pp in range(pairs_per_kv):
                p = g * pairs_per_kv + pp
                s_b.append(_dot_nt(qr[rs, p * LANES:(p + 1) * LANES], k_bd))
    kprev_ref[...] = kr[tm - SW_BLOCK:, :]
    vprev_ref[...] = vx[tm - SW_BLOCK:, :]

    pg = _dot(h, win_ref[:, b_w + c_w:b_w + c_w + 3 * d])
    proj_a = _dot(ya_ref[...].astype(BF16), wa_ref[...])

    p_c = []
    for j in range(n_xh):
        m = jnp.max(s_c[j], axis=-1, keepdims=True)
        e = jnp.exp(s_c[j] - m)
        p_c.append((e / jnp.sum(e, axis=-1, keepdims=True)).astype(BF16))
    p_b = []
    for sb in range(n_sub):
        for p in range(n_pairs):
            s2 = s_b[sb * n_pairs + p]
            probs = []
            for hh in range(2):
                s = jnp.where(masks[sb], s2[:, hh * 2 * SW_BLOCK:(hh + 1) * 2 * SW_BLOCK], NEG_INF)
                sink = sink_ref[0, 2 * p + hh]
                m = jnp.maximum(jnp.max(s, axis=-1, keepdims=True), sink)
                e = jnp.exp(s - m)
                den = jnp.sum(e, axis=-1, keepdims=True) + jnp.exp(sink - m)
                probs.append((e / den).astype(BF16))
            p_b.append(jnp.concatenate(probs, axis=1))
    o_c = [_dot(p_c[j], vm_ref[:, j * x_hd:(j + 1) * x_hd].astype(BF16)) for j in range(n_xh)]
    o_b = [[_dot(p_b[sb * n_pairs + p], v_bds[sb * n_kv + p // pairs_per_kv]) for p in range(n_pairs)]
           for sb in range(n_sub)]

    y_c = (jnp.concatenate(o_c, axis=1) * _silu(pc[:,x_w:2 * x_w])).astype(BF16)
    proj_c = _dot(y_c, wc_ref[...])
    y_b = jnp.concatenate([jnp.concatenate(o_b[sb], axis=1) for sb in range(n_sub)], axis=0)
    y_b = (y_b * _silu(pb[:,sw_w + 2 * kv_w:2 * sw_w + 2 * kv_w])).astype(BF16)
    proj_b = _dot(y_b, wb_ref[...])
    merged = _sigmoid(pg[:,0:d]) * proj_a
    merged = merged + _sigmoid(pg[:,2 * d:3 * d]) * proj_c
    merged = merged + _sigmoid(pg[:,d:2 * d]) * proj_b
    o_ref[...] = x_ref[...] + _dot(merged.astype(BF16), wo_ref[...])


def _attn_out(x2, ya, pos2, km, vm, params, bsz, t, m_len, sw_w, kv_w, x_w, x_hd):
    n, d = x2.shape
    nt = t // ROW_TILE
    row = lambda a: pl.BlockSpec((ROW_TILE, a.shape[1]), lambda b, i: (b * nt + i, 0))
    full = lambda a: pl.BlockSpec(a.shape, lambda b, i: (0,) * a.ndim, pipeline_mode=pl.Buffered(1))
    memspec = pl.BlockSpec((m_len, x_w), lambda b, i: (b, 0))
    in_specs = [row(x2), row(ya), row(pos2), memspec, memspec]
    for a in params:
        in_specs.append(full(a))
    in_specs[5 + 5] = pl.BlockSpec(memory_space=pltpu.SMEM)
    return pl.pallas_call(
        functools.partial(_attn_out_kernel, sw_w=sw_w, kv_w=kv_w, x_w=x_w, x_hd=x_hd),
        grid=(bsz, nt),
        in_specs=in_specs,
        out_specs=pl.BlockSpec((ROW_TILE, d), lambda b, i: (b * nt + i, 0)),
        out_shape=jax.ShapeDtypeStruct((n, d), F32),
        scratch_shapes=[
            pltpu.VMEM((SW_BLOCK, kv_w), F32),
            pltpu.VMEM((SW_BLOCK, kv_w), F32),
        ],
        compiler_params=pltpu.CompilerParams(
            dimension_semantics=("arbitrary", "arbitrary"), vmem_limit_bytes=VMEM_LIMIT),
        name="attn_out",
    )(x2, ya, pos2, km, vm, *params)


def kernel(x, mem, positions, norm_g, mem_norm_g, w_in, mu_rkv, mu_wa, w0, w2, a0, a2, k_k, k_a, r_k,
           lnx_g, lnx_b, q_norm_g, k_norm_g, sinks, xq_norm_g, xk_norm_g, w_mem_kv,
           w_proj_a, w_proj_b, w_proj_c, w_out):
    bsz, t, d = x.shape
    m_len = mem.shape[1]
    depth = w_in.shape[0]
    rw_w = w0.shape[1]
    lora = w2.shape[1]
    sw_w = w_proj_b.shape[1]
    x_w = w_proj_c.shape[1]
    x_hd = xq_norm_g.shape[1]
    kv_w = (w_in.shape[2] - (4 * rw_w + 2 * lora) - 2 * sw_w - 2 * x_w - 3 * d) // 2
    assert t % ROW_TILE == 0 and rw_w % LANES == 0 and sw_w % LANES == 0 and kv_w == LANES
    assert q_norm_g.shape[1] == HEAD and r_k.shape[2] == HEAD and 2 * lora == LANES
    widths = (4 * rw_w + 2 * lora, 2 * sw_w + 2 * kv_w, 2 * x_w, 3 * d)
    n = bsz * t
    x2 = x.reshape(n, d)
    mem2 = mem.reshape(bsz * m_len, d)
    pos2 = positions.reshape(n, 1)
    half = HEAD // 2
    inv = ROPE_THETA ** (-(jnp.arange(LANES) % half).astype(F32) / half)
    inv = inv.reshape(1, LANES)
    for l in range(depth):
        row = lambda a: a[l].reshape(1, -1)
        w_in_bf = w_in[l].astype(BF16)
        a_w = widths[0]
        (pa,) = _proj_in(x2, row(norm_g), w_in_bf[:, 0:a_w], widths[0:1])
        km, vm = _mem_kv(mem2, row(mem_norm_g), w_mem_kv[l].astype(BF16), row(xk_norm_g), m_len, x_w, x_hd)
        mu = jnp.concatenate([mu_rkv[l].reshape(1, -1), mu_wa[l].reshape(1, -1)], axis=1)
        zeros = jnp.zeros((lora, rw_w), F32)
        lora_w = jnp.concatenate([jnp.concatenate([w2[l], zeros], axis=1),
                                  jnp.concatenate([zeros, a2[l]], axis=1)], axis=0)
        lora_hi = lora_w.astype(BF16)
        lora_lo = (lora_w - lora_hi.astype(F32)).astype(BF16)
        rw_params = (mu, row(w0), row(a0), lora_hi, lora_lo, row(k_k), row(k_a), row(r_k),
                     row(lnx_g), row(lnx_b))
        ya = _rwkv(pa, bsz, t, rw_w, 2 * lora, rw_params)
        at_params = (row(norm_g), w_in_bf[:, a_w:], inv,
                     jnp.tile(row(q_norm_g), (1, sw_w // HEAD)), jnp.tile(row(k_norm_g), (1, kv_w // HEAD)),
                     row(sinks), row(xq_norm_g),
                     w_proj_a[l].astype(BF16), w_proj_b[l].astype(BF16), w_proj_c[l].astype(BF16),
                     w_out[l].astype(BF16))
        x2 = _attn_out(x2, ya, pos2, km, vm, at_params, bsz, t, m_len, sw_w, kv_w, x_w, x_hd)
    return x2.reshape(bsz, t, d)
```

```python
import functools
import math

import jax
import jax.numpy as jnp
import numpy as np
from jax import lax
from jax.experimental import pallas as pl
from jax.experimental.pallas import tpu as pltpu

F32 = jnp.float32
BF16 = jnp.bfloat16

RMS_EPS = 1e-6
LNX_EPS = 64e-5
L2_EPS = 1e-12
DECAY_SCALE = math.exp(-0.5)
ROPE_THETA = 10000.0
NEG_INF = -1e30

HEAD = 64
LANES = 128
CHUNK = 64
SW_BLOCK = 128
ROW_TILE = 256
ATTN_TILE = 512
VMEM_LIMIT = 56 * 1024 * 1024


def _dot(a, b):
    return jnp.dot(a, b, preferred_element_type=F32)


def _dot_nt(a, b):
    return lax.dot_general(a, b, (((1,), (1,)), ((), ())), preferred_element_type=F32)


def _dot_tn(a, b):
    return lax.dot_general(a, b, (((0,), (0,)), ((), ())), preferred_element_type=F32)


def _split2(x):
    hi = x.astype(BF16)
    lo = (x - hi.astype(F32)).astype(BF16)
    return hi, lo


def _split3(x):
    hi = x.astype(BF16)
    r1 = x - hi.astype(F32)
    mid = r1.astype(BF16)
    lo = (r1 - mid.astype(F32)).astype(BF16)
    return hi, mid, lo


def _dot_x3(a, b_hi, b_lo):
    a_hi, a_lo = _split2(a)
    return _dot(a_hi, b_hi) + (_dot(a_hi, b_lo) + _dot(a_lo, b_hi))


def _head_sums(x):
    lo_mask = lax.broadcasted_iota(jnp.int32, (1, LANES), 1) < HEAD
    outs = []
    for p in range(x.shape[1] // LANES):
        xs = x[:, p * LANES:(p + 1) * LANES]
        s_lo = jnp.sum(jnp.where(lo_mask, xs, 0.0), axis=-1, keepdims=True)
        s_hi = jnp.sum(jnp.where(lo_mask, 0.0, xs), axis=-1, keepdims=True)
        outs.append(jnp.where(lo_mask, s_lo, s_hi))
    return outs[0] if len(outs) == 1 else jnp.concatenate(outs, axis=1)


def _sigmoid(x):
    return 1.0 / (1.0 + jnp.exp(-x))


def _silu(x):
    return x * _sigmoid(x)


def _block_diag2(y, lo_mask):
    return jnp.concatenate([jnp.where(lo_mask, y, 0.0), jnp.where(lo_mask, 0.0, y)], axis=0)


def _mem_kv_kernel(mem_ref, g_ref, w_ref, kg_ref, k_ref, v_ref, *, xw, hd):
    m = mem_ref[...]
    ms = jnp.mean(m * m, axis=-1, keepdims=True)
    h = (m * lax.rsqrt(ms + RMS_EPS) * g_ref[...]).astype(BF16)
    kv = _dot(h, w_ref[...])
    v_ref[...] = kv[:, xw:]
    for j in range(xw // hd):
        kj = kv[:, j * hd:(j + 1) * hd]
        msk = jnp.mean(kj * kj, axis=-1, keepdims=True)
        k_ref[:, j * hd:(j + 1) * hd] = kj * lax.rsqrt(msk + RMS_EPS) * kg_ref[...]


def _mem_kv(mem2, g, w_bf, kg, m_len, xw, hd):
    n, d = mem2.shape
    return pl.pallas_call(
        functools.partial(_mem_kv_kernel, xw=xw, hd=hd),
        grid=(n // m_len,),
        in_specs=[
            pl.BlockSpec((m_len, d), lambda i: (i, 0)),
            pl.BlockSpec((1, d), lambda i: (0, 0)),
            pl.BlockSpec(w_bf.shape, lambda i: (0, 0)),
            pl.BlockSpec((1, hd), lambda i: (0, 0)),
        ],
        out_specs=[pl.BlockSpec((m_len, xw), lambda i: (i, 0))] * 2,
        out_shape=[jax.ShapeDtypeStruct((n, xw), F32)] * 2,
        compiler_params=pltpu.CompilerParams(
            dimension_semantics=("arbitrary",), vmem_limit_bytes=VMEM_LIMIT),
        name="mem_kv",
    )(mem2, g, w_bf, kg)


def _rwkv_kernel(x_ref, ng_ref, win_ref, mu_ref, w0_ref, a0_ref, lora_hi_ref, lora_lo_ref, kk_ref, ka_ref,
                 rk_ref, lng_ref, lnb_ref, ya_ref, pa_ref, pan_ref, carry_ref, s_ref, *, width, lora2):
    nb, tt, _ = x_ref.shape
    n_pairs = width // LANES
    mixw = 3 * width + lora2
    nrow = nb * CHUNK

    @pl.when(pl.program_id(0) == 0)
    def _():
        carry_ref[...] = jnp.zeros_like(carry_ref)
        s_ref[...] = jnp.zeros_like(s_ref)
        pa_ref[...] = jnp.zeros_like(pa_ref)

    ti = lax.broadcasted_iota(jnp.int32, (CHUNK, LANES), 0)
    li = lax.broadcasted_iota(jnp.int32, (CHUNK, LANES), 1)
    lo_mask = li < HEAD
    lj = jnp.where(lo_mask, li, li - HEAD)
    strict = ti > lj
    incl = ti >= lj
    eye = (ti == lj).astype(F32)
    ri = lax.broadcasted_iota(jnp.int32, (nrow, nrow), 0)
    ci = lax.broadcasted_iota(jnp.int32, (nrow, nrow), 1)
    tri = ((ri >= ci) & (jnp.bitwise_xor(ri, ci) < CHUNK)).astype(BF16)
    row_id = lax.broadcasted_iota(jnp.int32, (nrow, 1), 0)
    mu = mu_ref[...]
    lane_l = lax.broadcasted_iota(jnp.int32, (1, lora2), 1)

    def bd(y):
        return _block_diag2(y, lo_mask).astype(BF16)

    def prep(c):
        r0 = c * CHUNK
        u = jnp.concatenate([pa_ref[b, r0:r0 + CHUNK, 0:mixw] for b in range(nb)], axis=0)
        prev = pltpu.roll(u, 1, 0)
        for b in range(nb):
            before = carry_ref[b:b + 1, :] if c == 0 else pa_ref[b, r0 - 1:r0, 0:mixw]
            prev = jnp.where(row_id == b * CHUNK, before, prev)
        mixed = u + (prev - u) * mu
        r = mixed[:, 0:width]
        k = mixed[:, width:2 * width]
        v = mixed[:, 2 * width:3 * width]
        lo_ra = mixed[:, 3 * width:mixw]
        lo_in = jnp.where(lane_l < lora2 // 2, jnp.tanh(lo_ra), lo_ra)
        proj = _dot_x3(lo_in, lora_hi_ref[...], lora_lo_ref[...])
        lw = -DECAY_SCALE * _sigmoid(w0_ref[...] + proj[:, 0:width])
        a_sig = _sigmoid(a0_ref[...] + proj[:, width:2 * width])
        kk = k * kk_ref[...]
        kkn = kk * lax.rsqrt(jnp.maximum(_head_sums(kk * kk), L2_EPS * L2_EPS))
        k2 = k * (1.0 + (a_sig - 1.0) * ka_ref[...])
        bvec = kkn * a_sig
        l_hi, l_mid, l_lo = _split3(lw)
        cs = _dot(tri, l_hi) + (_dot(tri, l_mid) + _dot(tri, l_lo))
        return dict(r0=r0, r=r, v=v, lw=lw, kkn=kkn, k2=k2, bvec=bvec, cs=cs)

    def prep_tail(q):
        r0, r, v, lw, kkn, k2, bvec, cs = (q[n] for n in ("r0", "r", "v", "lw", "kkn", "k2", "bvec", "cs"))
        cs_last = [cs[(b + 1) * CHUNK - 1:(b + 1) * CHUNK, :] for b in range(nb)]
        cs_end = jnp.concatenate([jnp.broadcast_to(cl, (CHUNK, width)) for cl in cs_last], axis=0)
        w_last = [jnp.exp(cl) for cl in cs_last]
        e_in = jnp.exp(cs)
        e_neg = jnp.exp(-cs)
        e_tail = jnp.exp(cs_end - cs)
        gate = jnp.concatenate([_silu(pa_ref[b, r0:r0 + CHUNK, mixw:mixw + width]) for b in range(nb)], axis=0)
        return dict(r=r, k2=k2, v=v, w_last=w_last, gate=gate, rt=r * e_in, at=-kkn * jnp.exp(cs - lw),
                    kt=k2 * e_neg, bt=bvec * e_neg, kh=k2 * e_tail, bh=bvec * e_tail)

    n_ch = tt // CHUNK
    pre_head = [prep(c) for c in range(n_ch)]

    for b in range(nb):
        xin = x_ref[b]
        h = (xin * lax.rsqrt(jnp.mean(xin * xin, axis=-1, keepdims=True) + RMS_EPS) * ng_ref[...]).astype(BF16)
        pan_ref[b] = _dot(h, win_ref[:, 0:mixw + width])

    pre = [prep_tail(q) for q in pre_head]
    for b in range(nb):
        carry_ref[b:b + 1, :] = pa_ref[b, tt - 1:tt, 0:mixw]

    def blk(c, name, b, p):
        return pre[c][name][b * CHUNK:(b + 1) * CHUNK, p * LANES:(p + 1) * LANES]

    probs = [(c, b, p) for c in range(n_ch) for b in range(nb) for p in range(n_pairs)]
    idx = range(len(probs))
    lhs2 = [jnp.concatenate([blk(c, "at", b, p), blk(c, "rt", b, p)], axis=0).astype(BF16) for c, b, p in probs]
    g = [_dot_nt(lhs2[i], jnp.concatenate([bd(blk(c, "bt", b, p)), bd(blk(c, "kt", b, p))], axis=0))
         for i, (c, b, p) in enumerate(probs)]
    a_ab = [jnp.where(strict, g[i][0:CHUNK, 0:LANES], 0.0) for i in idx]
    a_ak = [jnp.where(strict, g[i][0:CHUNK, LANES:], 0.0) for i in idx]
    a_r = [jnp.concatenate([jnp.where(incl, g[i][CHUNK:, 0:LANES], 0.0),
                            jnp.where(incl, g[i][CHUNK:, LANES:], 0.0)], axis=1).astype(BF16) for i in idx]
    v_bd = [bd(blk(c, "v", b, p)) for c, b, p in probs]
    av = [_dot(a_ak[i].astype(BF16), v_bd[i]) for i in idx]
    x = [eye + a_ab[i] for i in idx]
    pw = [_dot(a_ab[i].astype(BF16), bd(a_ab[i])) for i in idx]
    for _ in range(4):
        px = [_dot(jnp.concatenate([pw[i], x[i]], axis=0).astype(BF16), bd(pw[i])) for i in idx]
        pw = [px[i][0:CHUNK] for i in idx]
        x = [x[i] + px[i][CHUNK:] for i in idx]
    x = [(x[i] + _dot(x[i].astype(BF16), bd(pw[i]))).astype(BF16) for i in idx]

    n_bp = nb * n_pairs
    s_cur = [s_ref[j] for j in range(n_bp)]
    inv_n = 1.0 / HEAD
    for c in range(n_ch):
        r0 = c * CHUNK
        ii = [c * n_bp + j for j in range(n_bp)]
        bp = [probs[i][1:] for i in ii]
        asrs = [_dot_nt(lhs2[i], bd(s_cur[j])) for j, i in enumerate(ii)]
        z = [asrs[j][0:CHUNK] + av[i] for j, i in enumerate(ii)]
        u_p = [_dot(x[i], bd(z[j])) for j, i in enumerate(ii)]
        d = [_dot_tn(jnp.concatenate([u_p[j], blk(c, "v", b, p)], axis=0).astype(BF16),
                     jnp.concatenate([blk(c, "bh", b, p), blk(c, "kh", b, p)], axis=0).astype(BF16))
             for j, (b, p) in enumerate(bp)]
        s_cur = [s_cur[j] * pre[c]["w_last"][b][:, p * LANES:(p + 1) * LANES]
                 + jnp.where(lo_mask, d[j][0:HEAD], d[j][HEAD:]) for j, (b, p) in enumerate(bp)]
        y = [asrs[j][CHUNK:] + _dot(a_r[i], jnp.concatenate([bd(u_p[j]), v_bd[i]], axis=0))
             for j, i in enumerate(ii)]
        y_all = jnp.concatenate(
            [jnp.concatenate([y[b * n_pairs + p] for p in range(n_pairs)], axis=1) for b in range(nb)], axis=0)
        yc = y_all - _head_sums(y_all) * inv_n
        var = _head_sums(yc * yc) * inv_n
        yn = yc * lax.rsqrt(var + LNX_EPS) * lng_ref[...] + lnb_ref[...]
        bonus = _head_sums(pre[c]["r"] * pre[c]["k2"] * rk_ref[...]) * pre[c]["v"]
        for b in range(nb):
            bs = slice(b * CHUNK, (b + 1) * CHUNK)
            ya_ref[b, r0:r0 + CHUNK, :] = (yn[bs] + bonus[bs]) * pre[c]["gate"][bs]
    for j in range(n_bp):
        s_ref[j] = s_cur[j]
    pa_ref[...] = pan_ref[...]


def _rwkv(x3, width, lora2, params):
    bsz, t, d = x3.shape
    nt = t // ROW_TILE
    a_w = 4 * width + lora2
    full = lambda a: pl.BlockSpec(a.shape, lambda i: (0,) * a.ndim, pipeline_mode=pl.Buffered(1))
    ya = pl.pallas_call(
        functools.partial(_rwkv_kernel, width=width, lora2=lora2),
        grid=(nt + 1,),
        in_specs=[pl.BlockSpec((bsz, ROW_TILE, d), lambda i: (0, jnp.minimum(i, nt - 1), 0))]
        + [full(a) for a in params],
        out_specs=pl.BlockSpec((bsz, ROW_TILE, width), lambda i: (0, jnp.maximum(i - 1, 0), 0)),
        out_shape=jax.ShapeDtypeStruct((bsz, t, width), F32),
        scratch_shapes=[
            pltpu.VMEM((bsz, ROW_TILE, a_w), F32),
            pltpu.VMEM((bsz, ROW_TILE, a_w), F32),
            pltpu.VMEM((bsz, 3 * width + lora2), F32),
            pltpu.VMEM((bsz * (width // LANES), HEAD, LANES), F32),
        ],
        compiler_params=pltpu.CompilerParams(
            dimension_semantics=("arbitrary",), vmem_limit_bytes=VMEM_LIMIT),
        name="rwkv",
    )(x3, *params)
    return ya.reshape(bsz * t, width)


def _attn_out_kernel(x_ref, ya_ref, pos_ref, km_ref, vm_ref, ng_ref, win_ref,
                     inv_ref, qg_ref, kg_ref, sink_ref, xqg_ref,
                     wa_ref, wb_ref, wc_ref, wo_ref, o_ref, kprev_ref, vprev_ref,
                     *, sw_w, kv_w, x_w, x_hd):
    tm = x_ref.shape[0]
    d = x_ref.shape[1]
    first = pl.program_id(1) == 0

    @pl.when(first)
    def _():
        kprev_ref[...] = jnp.zeros_like(kprev_ref)
        vprev_ref[...] = jnp.zeros_like(vprev_ref)


    xin = x_ref[...]
    h = (xin * lax.rsqrt(jnp.mean(xin * xin, axis=-1, keepdims=True) + RMS_EPS) * ng_ref[...]).astype(BF16)
    b_w = 2 * sw_w + 2 * kv_w
    c_w = 2 * x_w
    b_0 = win_ref.shape[1] - (b_w + c_w + 3 * d)
    c_0 = b_0 + b_w
    g_0 = c_0 + c_w
    pc = _dot(h, win_ref[:, c_0:g_0])
    pb = _dot(h, win_ref[:, b_0:c_0])

    half = HEAD // 2
    n_grp = LANES // half
    qrows = tm // n_grp
    pos = pos_ref[...].astype(F32)
    lane_grp = lax.broadcasted_iota(jnp.int32, (1, LANES), 1) // half
    pos_d = pos[0:qrows]
    for j in range(1, n_grp):
        pos_d = jnp.where(lane_grp == j, pos[j * qrows:(j + 1) * qrows], pos_d)
    ang_d = pos_d * inv_ref[...]
    cos_d = jnp.cos(ang_d)
    sin_d = jnp.sin(ang_d)

    def spread(tab):
        rolled = [tab] + [pltpu.roll(tab, half * k, 1) for k in range(1, n_grp)]
        quarters = []
        for j in range(n_grp):
            cj = rolled[(0 - j) % n_grp]
            for gl in range(1, n_grp):
                cj = jnp.where(lane_grp == gl, rolled[(gl - j) % n_grp], cj)
            quarters.append(cj)
        return jnp.concatenate(quarters, axis=0)

    cos_p = spread(cos_d)
    sin_p = spread(sin_d)

    def rope(xn):
        w = xn.shape[1]
        reps = w // LANES
        first_half = (lax.broadcasted_iota(jnp.int32, (1, w), 1) & (HEAD - 1)) < half
        cos_t = jnp.concatenate([cos_p] * reps, axis=1) if reps > 1 else cos_p
        sin_t = jnp.concatenate([sin_p] * reps, axis=1) if reps > 1 else sin_p
        rot = jnp.where(first_half, -pltpu.roll(xn, w - half, 1), pltpu.roll(xn, half, 1))
        return xn * cos_t + rot * sin_t

    q = pb[:,0:sw_w]
    qn = q * lax.rsqrt(_head_sums(q * q) * (1.0 / HEAD) + RMS_EPS) * (qg_ref[...] * (HEAD ** -0.5))
    qr = rope(qn).astype(BF16)
    kx = pb[:,sw_w:sw_w + kv_w]
    kn = kx * lax.rsqrt(_head_sums(kx * kx) * (1.0 / HEAD) + RMS_EPS) * kg_ref[...]
    kr = rope(kn)
    vx = pb[:,sw_w + kv_w:sw_w + 2 * kv_w]
    n_xh = x_w // x_hd
    xqn = []
    for j in range(n_xh):
        xq = pc[:,j * x_hd:(j + 1) * x_hd]
        xqn.append((xq * lax.rsqrt(jnp.mean(xq * xq, axis=-1, keepdims=True) + RMS_EPS)
                    * (xqg_ref[...] * (x_hd ** -0.5))).astype(BF16))

    lo_mask = lax.broadcasted_iota(jnp.int32, (1, LANES), 1) < HEAD
    qi = lax.broadcasted_iota(jnp.int32, (SW_BLOCK, 2 * SW_BLOCK), 0)
    kj = lax.broadcasted_iota(jnp.int32, (SW_BLOCK, 2 * SW_BLOCK), 1)
    allowed = (kj > qi) & (kj <= qi + SW_BLOCK)
    n_sub = tm // SW_BLOCK
    n_kv = kv_w // HEAD
    n_pairs = sw_w // LANES
    pairs_per_kv = n_pairs // n_kv

    s_c = [_dot_nt(xqn[j], km_ref[:, j * x_hd:(j + 1) * x_hd].astype(BF16)) for j in range(n_xh)]
    v_bds, masks, s_b = [], [], []
    for sb in range(n_sub):
        rs = slice(sb * SW_BLOCK, (sb + 1) * SW_BLOCK)
        if sb == 0:
            kband = jnp.concatenate([kprev_ref[...], kr[rs]], axis=0)
            vband = jnp.concatenate([vprev_ref[...], vx[rs]], axis=0)
            masks.append(allowed & (kj >= jnp.where(first, SW_BLOCK, 0)))
        else:
            kband = kr[(sb - 1) * SW_BLOCK:(sb + 1) * SW_BLOCK]
            vband = vx[(sb - 1) * SW_BLOCK:(sb + 1) * SW_BLOCK]
            masks.append(allowed)
        kband_sw = pltpu.roll(kband, HEAD, 1)
        vband_sw = pltpu.roll(vband, HEAD, 1)
        for g in range(n_kv):
            k_lo = kband if g == 0 else kband_sw
            k_hi = kband_sw if g == 0 else kband
            v_lo = vband if g == 0 else vband_sw
            v_hi = vband_sw if g == 0 else vband
            k_bd = jnp.concatenate([jnp.where(lo_mask, k_lo, 0.0), jnp.where(lo_mask, 0.0, k_hi)],
                                   axis=0).astype(BF16)
            v_bds.append(jnp.concatenate([jnp.where(lo_mask, v_lo, 0.0), jnp.where(lo_mask, 0.0, v_hi)],
                                         axis=0).astype(BF16))
            for pp in range(pairs_per_kv):
                p = g * pairs_per_kv + pp
                s_b.append(_dot_nt(qr[rs, p * LANES:(p + 1) * LANES], k_bd))
    kprev_ref[...] = kr[tm - SW_BLOCK:, :]
    vprev_ref[...] = vx[tm - SW_BLOCK:, :]

    pg = _dot(h, win_ref[:, g_0:g_0 + 3 * d])
    proj_a = _dot(ya_ref[...].astype(BF16), wa_ref[...])

    p_c = []
    for j in range(n_xh):
        m = jnp.max(s_c[j], axis=-1, keepdims=True)
        e = jnp.exp(s_c[j] - m)
        p_c.append((e / jnp.sum(e, axis=-1, keepdims=True)).astype(BF16))
    p_b = []
    for sb in range(n_sub):
        for p in range(n_pairs):
            s2 = s_b[sb * n_pairs + p]
            probs = []
            for hh in range(2):
                s = jnp.where(masks[sb], s2[:, hh * 2 * SW_BLOCK:(hh + 1) * 2 * SW_BLOCK], NEG_INF)
                sink = sink_ref[0, 2 * p + hh]
                m = jnp.maximum(jnp.max(s, axis=-1, keepdims=True), sink)
                e = jnp.exp(s - m)
                den = jnp.sum(e, axis=-1, keepdims=True) + jnp.exp(sink - m)
                probs.append((e / den).astype(BF16))
            p_b.append(jnp.concatenate(probs, axis=1))
    o_c = [_dot(p_c[j], vm_ref[:, j * x_hd:(j + 1) * x_hd].astype(BF16)) for j in range(n_xh)]
    o_b = [[_dot(p_b[sb * n_pairs + p], v_bds[sb * n_kv + p // pairs_per_kv]) for p in range(n_pairs)]
           for sb in range(n_sub)]

    y_c = (jnp.concatenate(o_c, axis=1) * _silu(pc[:,x_w:2 * x_w])).astype(BF16)
    proj_c = _dot(y_c, wc_ref[...])
    y_b = jnp.concatenate([jnp.concatenate(o_b[sb], axis=1) for sb in range(n_sub)], axis=0)
    y_b = (y_b * _silu(pb[:,sw_w + 2 * kv_w:2 * sw_w + 2 * kv_w])).astype(BF16)
    proj_b = _dot(y_b, wb_ref[...])
    merged = _sigmoid(pg[:,0:d]) * proj_a
    merged = merged + _sigmoid(pg[:,2 * d:3 * d]) * proj_c
    merged = merged + _sigmoid(pg[:,d:2 * d]) * proj_b
    o_ref[...] = x_ref[...] + _dot(merged.astype(BF16), wo_ref[...])


def _attn_out(x2, ya, pos2, km, vm, params, bsz, t, m_len, sw_w, kv_w, x_w, x_hd):
    n, d = x2.shape
    nt = t // ATTN_TILE
    row = lambda a: pl.BlockSpec((ATTN_TILE, a.shape[1]), lambda b, i: (b * nt + i, 0))
    full = lambda a: pl.BlockSpec(a.shape, lambda b, i: (0,) * a.ndim, pipeline_mode=pl.Buffered(1))
    memspec = pl.BlockSpec((m_len, x_w), lambda b, i: (b, 0))
    in_specs = [row(x2), row(ya), row(pos2), memspec, memspec]
    for a in params:
        in_specs.append(full(a))
    in_specs[5 + 5] = pl.BlockSpec(memory_space=pltpu.SMEM)
    return pl.pallas_call(
        functools.partial(_attn_out_kernel, sw_w=sw_w, kv_w=kv_w, x_w=x_w, x_hd=x_hd),
        grid=(bsz, nt),
        in_specs=in_specs,
        out_specs=pl.BlockSpec((ATTN_TILE, d), lambda b, i: (b * nt + i, 0)),
        out_shape=jax.ShapeDtypeStruct((n, d), F32),
        scratch_shapes=[
            pltpu.VMEM((SW_BLOCK, kv_w), F32),
            pltpu.VMEM((SW_BLOCK, kv_w), F32),
        ],
        compiler_params=pltpu.CompilerParams(
            dimension_semantics=("arbitrary", "arbitrary"), vmem_limit_bytes=VMEM_LIMIT),
        name="attn_out",
    )(x2, ya, pos2, km, vm, *params)


def kernel(x, mem, positions, norm_g, mem_norm_g, w_in, mu_rkv, mu_wa, w0, w2, a0, a2, k_k, k_a, r_k,
           lnx_g, lnx_b, q_norm_g, k_norm_g, sinks, xq_norm_g, xk_norm_g, w_mem_kv,
           w_proj_a, w_proj_b, w_proj_c, w_out):
    bsz, t, d = x.shape
    m_len = mem.shape[1]
    depth = w_in.shape[0]
    rw_w = w0.shape[1]
    lora = w2.shape[1]
    sw_w = w_proj_b.shape[1]
    x_w = w_proj_c.shape[1]
    x_hd = xq_norm_g.shape[1]
    kv_w = (w_in.shape[2] - (4 * rw_w + 2 * lora) - 2 * sw_w - 2 * x_w - 3 * d) // 2
    assert t % ROW_TILE == 0 and t % ATTN_TILE == 0 and rw_w % LANES == 0 and sw_w % LANES == 0 and kv_w == LANES
    assert q_norm_g.shape[1] == HEAD and r_k.shape[2] == HEAD and 2 * lora == LANES
    n = bsz * t
    x2 = x.reshape(n, d)
    mem2 = mem.reshape(bsz * m_len, d)
    pos2 = positions.reshape(n, 1)
    half = HEAD // 2
    inv = ROPE_THETA ** (-(jnp.arange(LANES) % half).astype(F32) / half)
    inv = inv.reshape(1, LANES)
    for l in range(depth):
        row = lambda a: a[l].reshape(1, -1)
        w_in_bf = w_in[l].astype(BF16)
        km, vm = _mem_kv(mem2, row(mem_norm_g), w_mem_kv[l].astype(BF16), row(xk_norm_g), m_len, x_w, x_hd)
        mu = jnp.concatenate([mu_rkv[l].reshape(1, -1), mu_wa[l].reshape(1, -1)], axis=1)
        zeros = jnp.zeros((lora, rw_w), F32)
        lora_w = jnp.concatenate([jnp.concatenate([w2[l], zeros], axis=1),
                                  jnp.concatenate([zeros, a2[l]], axis=1)], axis=0)
        lora_hi = lora_w.astype(BF16)
        lora_lo = (lora_w - lora_hi.astype(F32)).astype(BF16)
        rw_params = (row(norm_g), w_in_bf, mu, row(w0), row(a0), lora_hi, lora_lo, row(k_k), row(k_a), row(r_k),
                     row(lnx_g), row(lnx_b))
        ya = _rwkv(x2.reshape(bsz, t, d), rw_w, 2 * lora, rw_params)
        at_params = (row(norm_g), w_in_bf, inv,
                     jnp.tile(row(q_norm_g), (1, sw_w // HEAD)), jnp.tile(row(k_norm_g), (1, kv_w // HEAD)),
                     row(sinks), row(xq_norm_g),
                     w_proj_a[l].astype(BF16), w_proj_b[l].astype(BF16), w_proj_c[l].astype(BF16),
                     w_out[l].astype(BF16))
        x2 = _attn_out(x2, ya, pos2, km, vm, at_params, bsz, t, m_len, sw_w, kv_w, x_w, x_hd)
    return x2.reshape(bsz, t, d)
```

```python
import functools
import math

import jax
import jax.numpy as jnp
import numpy as np
from jax import lax
from jax.experimental import pallas as pl
from jax.experimental.pallas import tpu as pltpu

F32 = jnp.float32
BF16 = jnp.bfloat16

RMS_EPS = 1e-6
LNX_EPS = 64e-5
L2_EPS = 1e-12
DECAY_SCALE = math.exp(-0.5)
ROPE_THETA = 10000.0
NEG_INF = -1e30

HEAD = 64
LANES = 128
CHUNK = 64
SW_BLOCK = 128
ROW_TILE = 256
ATTN_TILE = 512
VMEM_LIMIT = 56 * 1024 * 1024


def _dot(a, b):
    return jnp.dot(a, b, preferred_element_type=F32)


def _dot_nt(a, b):
    return lax.dot_general(a, b, (((1,), (1,)), ((), ())), preferred_element_type=F32)


def _dot_tn(a, b):
    return lax.dot_general(a, b, (((0,), (0,)), ((), ())), preferred_element_type=F32)


def _split2(x):
    hi = x.astype(BF16)
    lo = (x - hi.astype(F32)).astype(BF16)
    return hi, lo


def _split3(x):
    hi = x.astype(BF16)
    r1 = x - hi.astype(F32)
    mid = r1.astype(BF16)
    lo = (r1 - mid.astype(F32)).astype(BF16)
    return hi, mid, lo


def _dot_x3(a, b_hi, b_lo):
    a_hi, a_lo = _split2(a)
    return _dot(a_hi, b_hi) + (_dot(a_hi, b_lo) + _dot(a_lo, b_hi))


def _head_sums(x):
    lo_mask = lax.broadcasted_iota(jnp.int32, (1, LANES), 1) < HEAD
    outs = []
    for p in range(x.shape[1] // LANES):
        xs = x[:, p * LANES:(p + 1) * LANES]
        s_lo = jnp.sum(jnp.where(lo_mask, xs, 0.0), axis=-1, keepdims=True)
        s_hi = jnp.sum(jnp.where(lo_mask, 0.0, xs), axis=-1, keepdims=True)
        outs.append(jnp.where(lo_mask, s_lo, s_hi))
    return outs[0] if len(outs) == 1 else jnp.concatenate(outs, axis=1)


def _sigmoid(x):
    return 0.5 * jnp.tanh(0.5 * x) + 0.5


def _silu(x):
    return x * _sigmoid(x)


def _block_diag2(y, lo_mask):
    return jnp.concatenate([jnp.where(lo_mask, y, 0.0), jnp.where(lo_mask, 0.0, y)], axis=0)


def _mem_kv_kernel(mem_ref, g_ref, w_ref, kg_ref, k_ref, v_ref, *, xw, hd):
    m = mem_ref[...]
    ms = jnp.mean(m * m, axis=-1, keepdims=True)
    h = (m * lax.rsqrt(ms + RMS_EPS) * g_ref[...]).astype(BF16)
    kv = _dot(h, w_ref[...])
    v_ref[...] = kv[:, xw:]
    for j in range(xw // hd):
        kj = kv[:, j * hd:(j + 1) * hd]
        msk = jnp.mean(kj * kj, axis=-1, keepdims=True)
        k_ref[:, j * hd:(j + 1) * hd] = kj * lax.rsqrt(msk + RMS_EPS) * kg_ref[...]


def _mem_kv(mem2, g, w_bf, kg, m_len, xw, hd):
    n, d = mem2.shape
    return pl.pallas_call(
        functools.partial(_mem_kv_kernel, xw=xw, hd=hd),
        grid=(n // m_len,),
        in_specs=[
            pl.BlockSpec((m_len, d), lambda i: (i, 0)),
            pl.BlockSpec((1, d), lambda i: (0, 0)),
            pl.BlockSpec(w_bf.shape, lambda i: (0, 0)),
            pl.BlockSpec((1, hd), lambda i: (0, 0)),
        ],
        out_specs=[pl.BlockSpec((m_len, xw), lambda i: (i, 0))] * 2,
        out_shape=[jax.ShapeDtypeStruct((n, xw), F32)] * 2,
        compiler_params=pltpu.CompilerParams(
            dimension_semantics=("arbitrary",), vmem_limit_bytes=VMEM_LIMIT),
        name="mem_kv",
    )(mem2, g, w_bf, kg)


def _rwkv_kernel(x_ref, ng_ref, win_ref, tri_ref, mu_ref, w0_ref, a0_ref, lora_hi_ref, lora_lo_ref, kk_ref, ka_ref,
                 rk_ref, lng_ref, lnb_ref, ya_ref, pa_ref, pan_ref, carry_ref, s_ref, *, width, lora2):
    nb, tt, _ = x_ref.shape
    n_pairs = width // LANES
    mixw = 3 * width + lora2
    nrow = nb * CHUNK

    @pl.when(pl.program_id(0) == 0)
    def _():
        carry_ref[...] = jnp.zeros_like(carry_ref)
        s_ref[...] = jnp.zeros_like(s_ref)
        pa_ref[...] = jnp.zeros_like(pa_ref)

    ti = lax.broadcasted_iota(jnp.int32, (CHUNK, LANES), 0)
    li = lax.broadcasted_iota(jnp.int32, (CHUNK, LANES), 1)
    lo_mask = li < HEAD
    lj = jnp.where(lo_mask, li, li - HEAD)
    strict = ti > lj
    incl = ti >= lj
    eye = (ti == lj).astype(F32)
    tri = tri_ref[...]
    row_id = lax.broadcasted_iota(jnp.int32, (nrow, 1), 0)
    mu = mu_ref[...]
    lane_l = lax.broadcasted_iota(jnp.int32, (1, lora2), 1)

    def bd(y):
        return _block_diag2(y, lo_mask).astype(BF16)

    def prep(c):
        r0 = c * CHUNK
        u = jnp.concatenate([pa_ref[b, r0:r0 + CHUNK, 0:mixw] for b in range(nb)], axis=0)
        prev = pltpu.roll(u, 1, 0)
        for b in range(nb):
            before = carry_ref[b:b + 1, :] if c == 0 else pa_ref[b, r0 - 1:r0, 0:mixw]
            prev = jnp.where(row_id == b * CHUNK, before, prev)
        mixed = u + (prev - u) * mu
        r = mixed[:, 0:width]
        k = mixed[:, width:2 * width]
        v = mixed[:, 2 * width:3 * width]
        lo_ra = mixed[:, 3 * width:mixw]
        lo_in = jnp.where(lane_l < lora2 // 2, jnp.tanh(lo_ra), lo_ra)
        proj = _dot_x3(lo_in, lora_hi_ref[...], lora_lo_ref[...])
        lw = -DECAY_SCALE * _sigmoid(w0_ref[...] + proj[:, 0:width])
        a_sig = _sigmoid(a0_ref[...] + proj[:, width:2 * width])
        kk = k * kk_ref[...]
        kkn = kk * lax.rsqrt(jnp.maximum(_head_sums(kk * kk), L2_EPS * L2_EPS))
        k2 = k * (1.0 + (a_sig - 1.0) * ka_ref[...])
        bvec = kkn * a_sig
        l_hi, l_mid, l_lo = _split3(lw)
        cs = _dot(tri, l_hi) + (_dot(tri, l_mid) + _dot(tri, l_lo))
        return dict(r0=r0, r=r, v=v, lw=lw, kkn=kkn, k2=k2, bvec=bvec, cs=cs)

    def prep_tail(q):
        r0, r, v, lw, kkn, k2, bvec, cs = (q[n] for n in ("r0", "r", "v", "lw", "kkn", "k2", "bvec", "cs"))
        cs_last = [cs[(b + 1) * CHUNK - 1:(b + 1) * CHUNK, :] for b in range(nb)]
        cs_end = jnp.concatenate([jnp.broadcast_to(cl, (CHUNK, width)) for cl in cs_last], axis=0)
        w_last = [jnp.exp(cl) for cl in cs_last]
        e_in = jnp.exp(cs)
        e_neg = jnp.exp(-cs)
        e_tail = jnp.exp(cs_end - cs)
        gate = jnp.concatenate([_silu(pa_ref[b, r0:r0 + CHUNK, mixw:mixw + width]) for b in range(nb)], axis=0)
        return dict(r=r, k2=k2, v=v, w_last=w_last, gate=gate, rt=r * e_in, at=-kkn * jnp.exp(cs - lw),
                    kt=k2 * e_neg, bt=bvec * e_neg, kh=k2 * e_tail, bh=bvec * e_tail)

    n_ch = tt // CHUNK
    pre_head = [prep(c) for c in range(n_ch)]

    for b in range(nb):
        xin = x_ref[b]
        h = (xin * lax.rsqrt(jnp.mean(xin * xin, axis=-1, keepdims=True) + RMS_EPS) * ng_ref[...]).astype(BF16)
        pan_ref[b] = _dot(h, win_ref[:, 0:mixw + width])

    pre = [prep_tail(q) for q in pre_head]
    for b in range(nb):
        carry_ref[b:b + 1, :] = pa_ref[b, tt - 1:tt, 0:mixw]

    def blk(c, name, b, p):
        return pre[c][name][b * CHUNK:(b + 1) * CHUNK, p * LANES:(p + 1) * LANES]

    n_bp = nb * n_pairs
    ares = {}

    def phase_a(chunks):
        probs = [(c, b, p) for c in chunks for b in range(nb) for p in range(n_pairs)]
        idx = range(len(probs))
        lhs2 = [jnp.concatenate([blk(c, "at", b, p), blk(c, "rt", b, p)], axis=0).astype(BF16)
                for c, b, p in probs]
        g = [_dot_nt(lhs2[i], jnp.concatenate([bd(blk(c, "bt", b, p)), bd(blk(c, "kt", b, p))], axis=0))
             for i, (c, b, p) in enumerate(probs)]
        yield
        a_ab = [jnp.where(strict, g[i][0:CHUNK, 0:LANES], 0.0) for i in idx]
        a_ak = [jnp.where(strict, g[i][0:CHUNK, LANES:], 0.0) for i in idx]
        a_r = [jnp.concatenate([jnp.where(incl, g[i][CHUNK:, 0:LANES], 0.0),
                                jnp.where(incl, g[i][CHUNK:, LANES:], 0.0)], axis=1).astype(BF16) for i in idx]
        v_bd = [bd(blk(c, "v", b, p)) for c, b, p in probs]
        av = [_dot(a_ak[i].astype(BF16), v_bd[i]) for i in idx]
        yield
        x = [eye + a_ab[i] for i in idx]
        pw = [_dot(a_ab[i].astype(BF16), bd(a_ab[i])) for i in idx]
        yield
        for _ in range(4):
            px = [_dot(jnp.concatenate([pw[i], x[i]], axis=0).astype(BF16), bd(pw[i])) for i in idx]
            pw = [px[i][0:CHUNK] for i in idx]
            x = [x[i] + px[i][CHUNK:] for i in idx]
            yield
        for i, prob in enumerate(probs):
            ares[prob] = dict(lhs2=lhs2[i], a_r=a_r[i], v_bd=v_bd[i], av=av[i],
                              x=(x[i] + _dot(x[i].astype(BF16), bd(pw[i]))).astype(BF16))
        yield

    state = [s_ref[j] for j in range(n_bp)]

    def phase_b(c):
        r0 = c * CHUNK
        bp = [(b, p) for b in range(nb) for p in range(n_pairs)]
        res = [ares[(c, b, p)] for b, p in bp]
        asrs = [_dot_nt(res[j]["lhs2"], bd(state[j])) for j in range(n_bp)]
        yield
        z = [asrs[j][0:CHUNK] + res[j]["av"] for j in range(n_bp)]
        u_p = [_dot(res[j]["x"], bd(z[j])) for j in range(n_bp)]
        yield
        d = [_dot_tn(jnp.concatenate([u_p[j], blk(c, "v", b, p)], axis=0).astype(BF16),
                     jnp.concatenate([blk(c, "bh", b, p), blk(c, "kh", b, p)], axis=0).astype(BF16))
             for j, (b, p) in enumerate(bp)]
        for j, (b, p) in enumerate(bp):
            state[j] = (state[j] * pre[c]["w_last"][b][:, p * LANES:(p + 1) * LANES]
                        + jnp.where(lo_mask, d[j][0:HEAD], d[j][HEAD:]))
        yield
        y = [asrs[j][CHUNK:] + _dot(res[j]["a_r"], jnp.concatenate([bd(u_p[j]), res[j]["v_bd"]], axis=0))
             for j in range(n_bp)]
        y_all = jnp.concatenate(
            [jnp.concatenate([y[b * n_pairs + p] for p in range(n_pairs)], axis=1) for b in range(nb)], axis=0)
        inv_n = 1.0 / HEAD
        yc = y_all - _head_sums(y_all) * inv_n
        var = _head_sums(yc * yc) * inv_n
        yn = yc * lax.rsqrt(var + LNX_EPS) * lng_ref[...] + lnb_ref[...]
        bonus = _head_sums(pre[c]["r"] * pre[c]["k2"] * rk_ref[...]) * pre[c]["v"]
        for b in range(nb):
            bs = slice(b * CHUNK, (b + 1) * CHUNK)
            ya_ref[b, r0:r0 + CHUNK, :] = (yn[bs] + bonus[bs]) * pre[c]["gate"][bs]
        yield

    half_ch = n_ch // 2
    for _ in phase_a(range(half_ch)):
        pass

    def recurrence(chunks):
        for c in chunks:
            yield from phase_b(c)

    b_gen = recurrence(range(half_ch))
    for _ in phase_a(range(half_ch, n_ch)):
        next(b_gen, None)
    for _ in b_gen:
        pass
    for _ in recurrence(range(half_ch, n_ch)):
        pass
    for j in range(n_bp):
        s_ref[j] = state[j]
    pa_ref[...] = pan_ref[...]


def _rwkv(x3, width, lora2, params):
    bsz, t, d = x3.shape
    nt = t // ROW_TILE
    a_w = 4 * width + lora2
    full = lambda a: pl.BlockSpec(a.shape, lambda i: (0,) * a.ndim, pipeline_mode=pl.Buffered(1))
    ya = pl.pallas_call(
        functools.partial(_rwkv_kernel, width=width, lora2=lora2),
        grid=(nt + 1,),
        in_specs=[pl.BlockSpec((bsz, ROW_TILE, d), lambda i: (0, jnp.minimum(i, nt - 1), 0))]
        + [full(a) for a in params],
        out_specs=pl.BlockSpec((bsz, ROW_TILE, width), lambda i: (0, jnp.maximum(i - 1, 0), 0)),
        out_shape=jax.ShapeDtypeStruct((bsz, t, width), F32),
        scratch_shapes=[
            pltpu.VMEM((bsz, ROW_TILE, a_w), F32),
            pltpu.VMEM((bsz, ROW_TILE, a_w), F32),
            pltpu.VMEM((bsz, 3 * width + lora2), F32),
            pltpu.VMEM((bsz * (width // LANES), HEAD, LANES), F32),
        ],
        compiler_params=pltpu.CompilerParams(
            dimension_semantics=("arbitrary",), vmem_limit_bytes=VMEM_LIMIT),
        name="rwkv",
    )(x3, *params)
    return ya.reshape(bsz * t, width)


def _attn_out_kernel(x_ref, ya_ref, pos_ref, km_ref, vm_ref, ng_ref, win_ref,
                     inv_ref, qg_ref, kg_ref, sink_ref, xqg_ref,
                     wa_ref, wb_ref, wc_ref, wo_ref, o_ref, kprev_ref, vprev_ref,
                     *, sw_w, kv_w, x_w, x_hd):
    tm = x_ref.shape[0]
    d = x_ref.shape[1]
    first = pl.program_id(1) == 0

    @pl.when(first)
    def _():
        kprev_ref[...] = jnp.zeros_like(kprev_ref)
        vprev_ref[...] = jnp.zeros_like(vprev_ref)


    xin = x_ref[...]
    h = (xin * lax.rsqrt(jnp.mean(xin * xin, axis=-1, keepdims=True) + RMS_EPS) * ng_ref[...]).astype(BF16)
    b_w = 2 * sw_w + 2 * kv_w
    c_w = 2 * x_w
    b_0 = win_ref.shape[1] - (b_w + c_w + 3 * d)
    c_0 = b_0 + b_w
    g_0 = c_0 + c_w
    pc = _dot(h, win_ref[:, c_0:g_0])
    pb = _dot(h, win_ref[:, b_0:c_0])
    pg = _dot(h, win_ref[:, g_0:g_0 + 3 * d])

    half = HEAD // 2
    n_grp = LANES // half
    qrows = tm // n_grp
    pos = pos_ref[...].astype(F32)
    lane_grp = lax.broadcasted_iota(jnp.int32, (1, LANES), 1) // half
    pos_d = pos[0:qrows]
    for j in range(1, n_grp):
        pos_d = jnp.where(lane_grp == j, pos[j * qrows:(j + 1) * qrows], pos_d)
    ang_d = pos_d * inv_ref[...]
    cos_d = jnp.cos(ang_d)
    sin_d = jnp.sin(ang_d)

    def spread(tab):
        rolled = [tab] + [pltpu.roll(tab, half * k, 1) for k in range(1, n_grp)]
        quarters = []
        for j in range(n_grp):
            cj = rolled[(0 - j) % n_grp]
            for gl in range(1, n_grp):
                cj = jnp.where(lane_grp == gl, rolled[(gl - j) % n_grp], cj)
            quarters.append(cj)
        return jnp.concatenate(quarters, axis=0)

    cos_p = spread(cos_d)
    sin_p = spread(sin_d)

    def rope(xn):
        w = xn.shape[1]
        reps = w // LANES
        first_half = (lax.broadcasted_iota(jnp.int32, (1, w), 1) & (HEAD - 1)) < half
        cos_t = jnp.concatenate([cos_p] * reps, axis=1) if reps > 1 else cos_p
        sin_t = jnp.concatenate([sin_p] * reps, axis=1) if reps > 1 else sin_p
        rot = jnp.where(first_half, -pltpu.roll(xn, w - half, 1), pltpu.roll(xn, half, 1))
        return xn * cos_t + rot * sin_t

    q = pb[:,0:sw_w]
    qn = q * lax.rsqrt(_head_sums(q * q) * (1.0 / HEAD) + RMS_EPS) * (qg_ref[...] * (HEAD ** -0.5))
    qr = rope(qn).astype(BF16)
    kx = pb[:,sw_w:sw_w + kv_w]
    kn = kx * lax.rsqrt(_head_sums(kx * kx) * (1.0 / HEAD) + RMS_EPS) * kg_ref[...]
    kr = rope(kn)
    vx = pb[:,sw_w + kv_w:sw_w + 2 * kv_w]
    n_xh = x_w // x_hd
    xqn = []
    for j in range(n_xh):
        xq = pc[:,j * x_hd:(j + 1) * x_hd]
        xqn.append((xq * lax.rsqrt(jnp.mean(xq * xq, axis=-1, keepdims=True) + RMS_EPS)
                    * (xqg_ref[...] * (x_hd ** -0.5))).astype(BF16))

    lo_mask = lax.broadcasted_iota(jnp.int32, (1, LANES), 1) < HEAD
    qi = lax.broadcasted_iota(jnp.int32, (SW_BLOCK, 2 * SW_BLOCK), 0)
    kj = lax.broadcasted_iota(jnp.int32, (SW_BLOCK, 2 * SW_BLOCK), 1)
    allowed = (kj > qi) & (kj <= qi + SW_BLOCK)
    n_sub = tm // SW_BLOCK
    n_kv = kv_w // HEAD
    n_pairs = sw_w // LANES
    pairs_per_kv = n_pairs // n_kv

    s_c = [_dot_nt(xqn[j], km_ref[:, j * x_hd:(j + 1) * x_hd].astype(BF16)) for j in range(n_xh)]
    v_bds, masks, s_b = [], [], []
    for sb in range(n_sub):
        rs = slice(sb * SW_BLOCK, (sb + 1) * SW_BLOCK)
        if sb == 0:
            kband = jnp.concatenate([kprev_ref[...], kr[rs]], axis=0)
            vband = jnp.concatenate([vprev_ref[...], vx[rs]], axis=0)
            masks.append(allowed & (kj >= jnp.where(first, SW_BLOCK, 0)))
        else:
            kband = kr[(sb - 1) * SW_BLOCK:(sb + 1) * SW_BLOCK]
            vband = vx[(sb - 1) * SW_BLOCK:(sb + 1) * SW_BLOCK]
            masks.append(allowed)
        kband_sw = pltpu.roll(kband, HEAD, 1)
        vband_sw = pltpu.roll(vband, HEAD, 1)
        for g in range(n_kv):
            k_lo = kband if g == 0 else kband_sw
            k_hi = kband_sw if g == 0 else kband
            v_lo = vband if g == 0 else vband_sw
            v_hi = vband_sw if g == 0 else vband
            k_bd = jnp.concatenate([jnp.where(lo_mask, k_lo, 0.0), jnp.where(lo_mask, 0.0, k_hi)],
                                   axis=0).astype(BF16)
            v_bds.append(jnp.concatenate([jnp.where(lo_mask, v_lo, 0.0), jnp.where(lo_mask, 0.0, v_hi)],
                                         axis=0).astype(BF16))
            for pp in range(pairs_per_kv):
                p = g * pairs_per_kv + pp
                s_b.append(_dot_nt(qr[rs, p * LANES:(p + 1) * LANES], k_bd))
    kprev_ref[...] = kr[tm - SW_BLOCK:, :]
    vprev_ref[...] = vx[tm - SW_BLOCK:, :]

    proj_a = _dot(ya_ref[...].astype(BF16), wa_ref[...])

    p_c = []
    for j in range(n_xh):
        m = jnp.max(s_c[j], axis=-1, keepdims=True)
        e = jnp.exp(s_c[j] - m)
        p_c.append((e / jnp.sum(e, axis=-1, keepdims=True)).astype(BF16))
    p_b = []
    for sb in range(n_sub):
        for p in range(n_pairs):
            s2 = s_b[sb * n_pairs + p]
            probs = []
            for hh in range(2):
                s = jnp.where(masks[sb], s2[:, hh * 2 * SW_BLOCK:(hh + 1) * 2 * SW_BLOCK], NEG_INF)
                sink = sink_ref[0, 2 * p + hh]
                m = jnp.maximum(jnp.max(s, axis=-1, keepdims=True), sink)
                e = jnp.exp(s - m)
                den = jnp.sum(e, axis=-1, keepdims=True) + jnp.exp(sink - m)
                probs.append((e / den).astype(BF16))
            p_b.append(jnp.concatenate(probs, axis=1))
    o_c = [_dot(p_c[j], vm_ref[:, j * x_hd:(j + 1) * x_hd].astype(BF16)) for j in range(n_xh)]
    o_b = [[_dot(p_b[sb * n_pairs + p], v_bds[sb * n_kv + p // pairs_per_kv]) for p in range(n_pairs)]
           for sb in range(n_sub)]

    y_c = (jnp.concatenate(o_c, axis=1) * _silu(pc[:,x_w:2 * x_w])).astype(BF16)
    proj_c = _dot(y_c, wc_ref[...])
    y_b = jnp.concatenate([jnp.concatenate(o_b[sb], axis=1) for sb in range(n_sub)], axis=0)
    y_b = (y_b * _silu(pb[:,sw_w + 2 * kv_w:2 * sw_w + 2 * kv_w])).astype(BF16)
    proj_b = _dot(y_b, wb_ref[...])
    merged = _sigmoid(pg[:,0:d]) * proj_a
    merged = merged + _sigmoid(pg[:,2 * d:3 * d]) * proj_c
    merged = merged + _sigmoid(pg[:,d:2 * d]) * proj_b
    o_ref[...] = x_ref[...] + _dot(merged.astype(BF16), wo_ref[...])


def _attn_out(x2, ya, pos2, km, vm, params, bsz, t, m_len, sw_w, kv_w, x_w, x_hd):
    n, d = x2.shape
    nt = t // ATTN_TILE
    row = lambda a: pl.BlockSpec((ATTN_TILE, a.shape[1]), lambda b, i: (b * nt + i, 0))
    full = lambda a: pl.BlockSpec(a.shape, lambda b, i: (0,) * a.ndim, pipeline_mode=pl.Buffered(1))
    memspec = pl.BlockSpec((m_len, x_w), lambda b, i: (b, 0))
    in_specs = [row(x2), row(ya), row(pos2), memspec, memspec]
    for a in params:
        in_specs.append(full(a))
    in_specs[5 + 5] = pl.BlockSpec(memory_space=pltpu.SMEM)
    return pl.pallas_call(
        functools.partial(_attn_out_kernel, sw_w=sw_w, kv_w=kv_w, x_w=x_w, x_hd=x_hd),
        grid=(bsz, nt),
        in_specs=in_specs,
        out_specs=pl.BlockSpec((ATTN_TILE, d), lambda b, i: (b * nt + i, 0)),
        out_shape=jax.ShapeDtypeStruct((n, d), F32),
        scratch_shapes=[
            pltpu.VMEM((SW_BLOCK, kv_w), F32),
            pltpu.VMEM((SW_BLOCK, kv_w), F32),
        ],
        compiler_params=pltpu.CompilerParams(
            dimension_semantics=("arbitrary", "arbitrary"), vmem_limit_bytes=VMEM_LIMIT),
        name="attn_out",
    )(x2, ya, pos2, km, vm, *params)


def kernel(x, mem, positions, norm_g, mem_norm_g, w_in, mu_rkv, mu_wa, w0, w2, a0, a2, k_k, k_a, r_k,
           lnx_g, lnx_b, q_norm_g, k_norm_g, sinks, xq_norm_g, xk_norm_g, w_mem_kv,
           w_proj_a, w_proj_b, w_proj_c, w_out):
    bsz, t, d = x.shape
    m_len = mem.shape[1]
    depth = w_in.shape[0]
    rw_w = w0.shape[1]
    lora = w2.shape[1]
    sw_w = w_proj_b.shape[1]
    x_w = w_proj_c.shape[1]
    x_hd = xq_norm_g.shape[1]
    kv_w = (w_in.shape[2] - (4 * rw_w + 2 * lora) - 2 * sw_w - 2 * x_w - 3 * d) // 2
    assert t % ROW_TILE == 0 and t % ATTN_TILE == 0 and rw_w % LANES == 0 and sw_w % LANES == 0 and kv_w == LANES
    assert q_norm_g.shape[1] == HEAD and r_k.shape[2] == HEAD and 2 * lora == LANES
    n = bsz * t
    x2 = x.reshape(n, d)
    mem2 = mem.reshape(bsz * m_len, d)
    pos2 = positions.reshape(n, 1)
    half = HEAD // 2
    inv = ROPE_THETA ** (-(jnp.arange(LANES) % half).astype(F32) / half)
    inv = inv.reshape(1, LANES)
    for l in range(depth):
        row = lambda a: a[l].reshape(1, -1)
        w_in_bf = w_in[l].astype(BF16)
        km, vm = _mem_kv(mem2, row(mem_norm_g), w_mem_kv[l].astype(BF16), row(xk_norm_g), m_len, x_w, x_hd)
        mu = jnp.concatenate([mu_rkv[l].reshape(1, -1), mu_wa[l].reshape(1, -1)], axis=1)
        zeros = jnp.zeros((lora, rw_w), F32)
        lora_w = jnp.concatenate([jnp.concatenate([w2[l], zeros], axis=1),
                                  jnp.concatenate([zeros, a2[l]], axis=1)], axis=0)
        lora_hi = lora_w.astype(BF16)
        lora_lo = (lora_w - lora_hi.astype(F32)).astype(BF16)
        t_i = np.arange(bsz * CHUNK)
        tri = jnp.asarray((t_i[:, None] >= t_i[None, :]) & (t_i[:, None] // CHUNK == t_i[None, :] // CHUNK),
                          dtype=BF16)
        rw_params = (row(norm_g), w_in_bf, tri, mu, row(w0), row(a0), lora_hi, lora_lo, row(k_k), row(k_a), row(r_k),
                     row(lnx_g), row(lnx_b))
        ya = _rwkv(x2.reshape(bsz, t, d), rw_w, 2 * lora, rw_params)
        at_params = (row(norm_g), w_in_bf, inv,
                     jnp.tile(row(q_norm_g), (1, sw_w // HEAD)), jnp.tile(row(k_norm_g), (1, kv_w // HEAD)),
                     row(sinks), row(xq_norm_g),
                     w_proj_a[l].astype(BF16), w_proj_b[l].astype(BF16), w_proj_c[l].astype(BF16),
                     w_out[l].astype(BF16))
        x2 = _attn_out(x2, ya, pos2, km, vm, at_params, bsz, t, m_len, sw_w, kv_w, x_w, x_hd)
    return x2.reshape(bsz, t, d)
```

```python
import functools
import math

import jax
import jax.numpy as jnp
import numpy as np
from jax import lax
from jax.experimental import pallas as pl
from jax.experimental.pallas import tpu as pltpu

F32 = jnp.float32
BF16 = jnp.bfloat16

RMS_EPS = 1e-6
LNX_EPS = 64e-5
L2_EPS = 1e-12
DECAY_SCALE = math.exp(-0.5)
ROPE_THETA = 10000.0
NEG_INF = -1e30

HEAD = 64
LANES = 128
CHUNK = 64
SW_BLOCK = 128
ROW_TILE = 256
ATTN_TILE = 512
VMEM_LIMIT = 56 * 1024 * 1024


def _dot(a, b):
    return jnp.dot(a, b, preferred_element_type=F32)


def _dot_nt(a, b):
    return lax.dot_general(a, b, (((1,), (1,)), ((), ())), preferred_element_type=F32)


def _dot_tn(a, b):
    return lax.dot_general(a, b, (((0,), (0,)), ((), ())), preferred_element_type=F32)


def _split2(x):
    hi = x.astype(BF16)
    lo = (x - hi.astype(F32)).astype(BF16)
    return hi, lo


def _split3(x):
    hi = x.astype(BF16)
    r1 = x - hi.astype(F32)
    mid = r1.astype(BF16)
    lo = (r1 - mid.astype(F32)).astype(BF16)
    return hi, mid, lo


def _dot_x3(a, b_hi, b_lo):
    a_hi, a_lo = _split2(a)
    return _dot(a_hi, b_hi) + (_dot(a_hi, b_lo) + _dot(a_lo, b_hi))


def _head_sums(x):
    lo_mask = lax.broadcasted_iota(jnp.int32, (1, LANES), 1) < HEAD
    outs = []
    for p in range(x.shape[1] // LANES):
        xs = x[:, p * LANES:(p + 1) * LANES]
        s_lo = jnp.sum(jnp.where(lo_mask, xs, 0.0), axis=-1, keepdims=True)
        s_hi = jnp.sum(jnp.where(lo_mask, 0.0, xs), axis=-1, keepdims=True)
        outs.append(jnp.where(lo_mask, s_lo, s_hi))
    return outs[0] if len(outs) == 1 else jnp.concatenate(outs, axis=1)


def _sigmoid(x):
    return 0.5 * jnp.tanh(0.5 * x) + 0.5


def _silu(x):
    return x * _sigmoid(x)


def _block_diag2(y, lo_mask):
    return jnp.concatenate([jnp.where(lo_mask, y, 0.0), jnp.where(lo_mask, 0.0, y)], axis=0)


def _mem_kv_kernel(mem_ref, g_ref, w_ref, kg_ref, k_ref, v_ref, *, xw, hd):
    m = mem_ref[...]
    ms = jnp.mean(m * m, axis=-1, keepdims=True)
    h = (m * lax.rsqrt(ms + RMS_EPS) * g_ref[...]).astype(BF16)
    kv = _dot(h, w_ref[...])
    v_ref[...] = kv[:, xw:]
    for j in range(xw // hd):
        kj = kv[:, j * hd:(j + 1) * hd]
        msk = jnp.mean(kj * kj, axis=-1, keepdims=True)
        k_ref[:, j * hd:(j + 1) * hd] = kj * lax.rsqrt(msk + RMS_EPS) * kg_ref[...]


def _mem_kv(mem2, g, w_bf, kg, m_len, xw, hd):
    n, d = mem2.shape
    return pl.pallas_call(
        functools.partial(_mem_kv_kernel, xw=xw, hd=hd),
        grid=(n // m_len,),
        in_specs=[
            pl.BlockSpec((m_len, d), lambda i: (i, 0)),
            pl.BlockSpec((1, d), lambda i: (0, 0)),
            pl.BlockSpec(w_bf.shape, lambda i: (0, 0)),
            pl.BlockSpec((1, hd), lambda i: (0, 0)),
        ],
        out_specs=[pl.BlockSpec((m_len, xw), lambda i: (i, 0))] * 2,
        out_shape=[jax.ShapeDtypeStruct((n, xw), F32)] * 2,
        compiler_params=pltpu.CompilerParams(
            dimension_semantics=("arbitrary",), vmem_limit_bytes=VMEM_LIMIT),
        name="mem_kv",
    )(mem2, g, w_bf, kg)


def _rwkv_kernel(x_ref, ng_ref, win_ref, tri_ref, mu_ref, w0_ref, a0_ref, lora_hi_ref, lora_lo_ref, kk_ref, ka_ref,
                 rk_ref, lng_ref, lnb_ref, ya_ref, pa_ref, pan_ref, carry_ref, s_ref, *, width, lora2):
    nb, tt, _ = x_ref.shape
    n_pairs = width // LANES
    mixw = 3 * width + lora2
    nrow = nb * CHUNK

    @pl.when(pl.program_id(0) == 0)
    def _():
        carry_ref[...] = jnp.zeros_like(carry_ref)
        s_ref[...] = jnp.zeros_like(s_ref)
        pa_ref[...] = jnp.zeros_like(pa_ref)

    ti = lax.broadcasted_iota(jnp.int32, (CHUNK, LANES), 0)
    li = lax.broadcasted_iota(jnp.int32, (CHUNK, LANES), 1)
    lo_mask = li < HEAD
    lj = jnp.where(lo_mask, li, li - HEAD)
    strict = ti > lj
    incl = ti >= lj
    eye = (ti == lj).astype(F32)
    tri = tri_ref[...]
    row_id = lax.broadcasted_iota(jnp.int32, (nrow, 1), 0)
    mu = mu_ref[...]
    lane_l = lax.broadcasted_iota(jnp.int32, (1, lora2), 1)

    def bd(y):
        return _block_diag2(y, lo_mask).astype(BF16)

    def prep(c):
        r0 = c * CHUNK
        u = jnp.concatenate([pa_ref[b, r0:r0 + CHUNK, 0:mixw] for b in range(nb)], axis=0)
        prev = pltpu.roll(u, 1, 0)
        for b in range(nb):
            before = carry_ref[b:b + 1, :] if c == 0 else pa_ref[b, r0 - 1:r0, 0:mixw]
            prev = jnp.where(row_id == b * CHUNK, before, prev)
        mixed = u + (prev - u) * mu
        r = mixed[:, 0:width]
        k = mixed[:, width:2 * width]
        v = mixed[:, 2 * width:3 * width]
        lo_ra = mixed[:, 3 * width:mixw]
        lo_in = jnp.where(lane_l < lora2 // 2, jnp.tanh(lo_ra), lo_ra)
        proj = _dot_x3(lo_in, lora_hi_ref[...], lora_lo_ref[...])
        lw = -DECAY_SCALE * _sigmoid(w0_ref[...] + proj[:, 0:width])
        a_sig = _sigmoid(a0_ref[...] + proj[:, width:2 * width])
        kk = k * kk_ref[...]
        kkn = kk * lax.rsqrt(jnp.maximum(_head_sums(kk * kk), L2_EPS * L2_EPS))
        k2 = k * (1.0 + (a_sig - 1.0) * ka_ref[...])
        bvec = kkn * a_sig
        l_hi, l_mid, l_lo = _split3(lw)
        cs = _dot(tri, l_hi) + (_dot(tri, l_mid) + _dot(tri, l_lo))
        return dict(r0=r0, r=r, v=v, lw=lw, kkn=kkn, k2=k2, bvec=bvec, cs=cs)

    def prep_tail(q):
        r0, r, v, lw, kkn, k2, bvec, cs = (q[n] for n in ("r0", "r", "v", "lw", "kkn", "k2", "bvec", "cs"))
        cs_last = [cs[(b + 1) * CHUNK - 1:(b + 1) * CHUNK, :] for b in range(nb)]
        cs_end = jnp.concatenate([jnp.broadcast_to(cl, (CHUNK, width)) for cl in cs_last], axis=0)
        w_last = [jnp.exp(cl) for cl in cs_last]
        e_in = jnp.exp(cs)
        e_neg = jnp.exp(-cs)
        e_tail = jnp.exp(cs_end - cs)
        gate = jnp.concatenate([_silu(pa_ref[b, r0:r0 + CHUNK, mixw:mixw + width]) for b in range(nb)], axis=0)
        return dict(r=r, k2=k2, v=v, w_last=w_last, gate=gate, rt=r * e_in, at=-kkn * jnp.exp(cs - lw),
                    kt=k2 * e_neg, bt=bvec * e_neg, kh=k2 * e_tail, bh=bvec * e_tail)

    n_ch = tt // CHUNK
    pre_head = [prep(c) for c in range(n_ch)]

    for b in range(nb):
        xin = x_ref[b]
        h = (xin * lax.rsqrt(jnp.mean(xin * xin, axis=-1, keepdims=True) + RMS_EPS) * ng_ref[...]).astype(BF16)
        pan_ref[b] = _dot(h, win_ref[:, 0:mixw + width])

    pre = [prep_tail(q) for q in pre_head]
    for b in range(nb):
        carry_ref[b:b + 1, :] = pa_ref[b, tt - 1:tt, 0:mixw]

    def blk(c, name, b, p):
        return pre[c][name][b * CHUNK:(b + 1) * CHUNK, p * LANES:(p + 1) * LANES]

    n_bp = nb * n_pairs
    ares = {}

    def phase_a(chunks):
        probs = [(c, b, p) for c in chunks for b in range(nb) for p in range(n_pairs)]
        idx = range(len(probs))
        lhs2 = [jnp.concatenate([blk(c, "at", b, p), blk(c, "rt", b, p)], axis=0).astype(BF16)
                for c, b, p in probs]
        g = [_dot_nt(lhs2[i], jnp.concatenate([bd(blk(c, "bt", b, p)), bd(blk(c, "kt", b, p))], axis=0))
             for i, (c, b, p) in enumerate(probs)]
        yield
        a_ab = [jnp.where(strict, g[i][0:CHUNK, 0:LANES], 0.0) for i in idx]
        a_ak = [jnp.where(strict, g[i][0:CHUNK, LANES:], 0.0) for i in idx]
        a_r = [jnp.concatenate([jnp.where(incl, g[i][CHUNK:, 0:LANES], 0.0),
                                jnp.where(incl, g[i][CHUNK:, LANES:], 0.0)], axis=1).astype(BF16) for i in idx]
        v_bd = [bd(blk(c, "v", b, p)) for c, b, p in probs]
        av = [_dot(a_ak[i].astype(BF16), v_bd[i]) for i in idx]
        yield
        x = [eye + a_ab[i] for i in idx]
        pw = [_dot(a_ab[i].astype(BF16), bd(a_ab[i])) for i in idx]
        yield
        for _ in range(4):
            px = [_dot(jnp.concatenate([pw[i], x[i]], axis=0).astype(BF16), bd(pw[i])) for i in idx]
            pw = [px[i][0:CHUNK] for i in idx]
            x = [x[i] + px[i][CHUNK:] for i in idx]
            yield
        for i, prob in enumerate(probs):
            ares[prob] = dict(lhs2=lhs2[i], a_r=a_r[i], v_bd=v_bd[i], av=av[i],
                              x=(x[i] + _dot(x[i].astype(BF16), bd(pw[i]))).astype(BF16))
        yield

    state = [s_ref[j] for j in range(n_bp)]

    def phase_b(c):
        r0 = c * CHUNK
        bp = [(b, p) for b in range(nb) for p in range(n_pairs)]
        res = [ares[(c, b, p)] for b, p in bp]
        asrs = [_dot_nt(res[j]["lhs2"], bd(state[j])) for j in range(n_bp)]
        yield
        z = [asrs[j][0:CHUNK] + res[j]["av"] for j in range(n_bp)]
        u_p = [_dot(res[j]["x"], bd(z[j])) for j in range(n_bp)]
        yield
        d = [_dot_tn(jnp.concatenate([u_p[j], blk(c, "v", b, p)], axis=0).astype(BF16),
                     jnp.concatenate([blk(c, "bh", b, p), blk(c, "kh", b, p)], axis=0).astype(BF16))
             for j, (b, p) in enumerate(bp)]
        for j, (b, p) in enumerate(bp):
            state[j] = (state[j] * pre[c]["w_last"][b][:, p * LANES:(p + 1) * LANES]
                        + jnp.where(lo_mask, d[j][0:HEAD], d[j][HEAD:]))
        yield
        y = [asrs[j][CHUNK:] + _dot(res[j]["a_r"], jnp.concatenate([bd(u_p[j]), res[j]["v_bd"]], axis=0))
             for j in range(n_bp)]
        y_all = jnp.concatenate(
            [jnp.concatenate([y[b * n_pairs + p] for p in range(n_pairs)], axis=1) for b in range(nb)], axis=0)
        inv_n = 1.0 / HEAD
        yc = y_all - _head_sums(y_all) * inv_n
        var = _head_sums(yc * yc) * inv_n
        yn = yc * lax.rsqrt(var + LNX_EPS) * lng_ref[...] + lnb_ref[...]
        bonus = _head_sums(pre[c]["r"] * pre[c]["k2"] * rk_ref[...]) * pre[c]["v"]
        for b in range(nb):
            bs = slice(b * CHUNK, (b + 1) * CHUNK)
            ya_ref[b, r0:r0 + CHUNK, :] = (yn[bs] + bonus[bs]) * pre[c]["gate"][bs]
        yield

    half_ch = n_ch // 2
    for _ in phase_a(range(half_ch)):
        pass

    def recurrence(chunks):
        for c in chunks:
            yield from phase_b(c)

    b_gen = recurrence(range(half_ch))
    for _ in phase_a(range(half_ch, n_ch)):
        next(b_gen, None)
    for _ in b_gen:
        pass
    for _ in recurrence(range(half_ch, n_ch)):
        pass
    for j in range(n_bp):
        s_ref[j] = state[j]
    pa_ref[...] = pan_ref[...]


def _rwkv(x3, width, lora2, params):
    bsz, t, d = x3.shape
    nt = t // ROW_TILE
    a_w = 4 * width + lora2
    full = lambda a: pl.BlockSpec(a.shape, lambda i: (0,) * a.ndim, pipeline_mode=pl.Buffered(1))
    ya = pl.pallas_call(
        functools.partial(_rwkv_kernel, width=width, lora2=lora2),
        grid=(nt + 1,),
        in_specs=[pl.BlockSpec((bsz, ROW_TILE, d), lambda i: (0, jnp.minimum(i, nt - 1), 0))]
        + [full(a) for a in params],
        out_specs=pl.BlockSpec((bsz, ROW_TILE, width), lambda i: (0, jnp.maximum(i - 1, 0), 0)),
        out_shape=jax.ShapeDtypeStruct((bsz, t, width), F32),
        scratch_shapes=[
            pltpu.VMEM((bsz, ROW_TILE, a_w), F32),
            pltpu.VMEM((bsz, ROW_TILE, a_w), F32),
            pltpu.VMEM((bsz, 3 * width + lora2), F32),
            pltpu.VMEM((bsz * (width // LANES), HEAD, LANES), F32),
        ],
        compiler_params=pltpu.CompilerParams(
            dimension_semantics=("arbitrary",), vmem_limit_bytes=VMEM_LIMIT),
        name="rwkv",
    )(x3, *params)
    return ya.reshape(bsz * t, width)


def _attn_out_kernel(x_ref, ya_ref, pos_ref, km_ref, vm_ref, ng_ref, win_ref,
                     inv_ref, qg_ref, kg_ref, sink_ref, xqg_ref,
                     wa_ref, wb_ref, wc_ref, wo_ref, o_ref, kprev_ref, vprev_ref,
                     *, sw_w, kv_w, x_w, x_hd):
    tm = x_ref.shape[0]
    d = x_ref.shape[1]
    first = pl.program_id(1) == 0

    @pl.when(first)
    def _():
        kprev_ref[...] = jnp.zeros_like(kprev_ref)
        vprev_ref[...] = jnp.zeros_like(vprev_ref)


    xin = x_ref[...]
    h = (xin * lax.rsqrt(jnp.mean(xin * xin, axis=-1, keepdims=True) + RMS_EPS) * ng_ref[...]).astype(BF16)
    b_w = 2 * sw_w + 2 * kv_w
    c_w = 2 * x_w
    b_0 = win_ref.shape[1] - (b_w + c_w + 3 * d)
    c_0 = b_0 + b_w
    g_0 = c_0 + c_w
    pc = _dot(h, win_ref[:, c_0:g_0])
    n_xh = x_w // x_hd
    xqn = []
    for j in range(n_xh):
        xq = pc[:,j * x_hd:(j + 1) * x_hd]
        xqn.append((xq * lax.rsqrt(jnp.mean(xq * xq, axis=-1, keepdims=True) + RMS_EPS)
                    * (xqg_ref[...] * (x_hd ** -0.5))).astype(BF16))
    s_c = [_dot_nt(xqn[j], km_ref[:, j * x_hd:(j + 1) * x_hd].astype(BF16)) for j in range(n_xh)]
    pb = _dot(h, win_ref[:, b_0:c_0])
    pg_a = _dot(h, win_ref[:, g_0:g_0 + d])

    half = HEAD // 2
    n_grp = LANES // half
    qrows = tm // n_grp
    pos = pos_ref[...].astype(F32)
    lane_grp = lax.broadcasted_iota(jnp.int32, (1, LANES), 1) // half
    pos_d = pos[0:qrows]
    for j in range(1, n_grp):
        pos_d = jnp.where(lane_grp == j, pos[j * qrows:(j + 1) * qrows], pos_d)
    ang_d = pos_d * inv_ref[...]
    cos_d = jnp.cos(ang_d)
    sin_d = jnp.sin(ang_d)

    def spread(tab):
        rolled = [tab] + [pltpu.roll(tab, half * k, 1) for k in range(1, n_grp)]
        quarters = []
        for j in range(n_grp):
            cj = rolled[(0 - j) % n_grp]
            for gl in range(1, n_grp):
                cj = jnp.where(lane_grp == gl, rolled[(gl - j) % n_grp], cj)
            quarters.append(cj)
        return jnp.concatenate(quarters, axis=0)

    cos_p = spread(cos_d)
    sin_p = spread(sin_d)

    def rope(xn):
        w = xn.shape[1]
        reps = w // LANES
        first_half = (lax.broadcasted_iota(jnp.int32, (1, w), 1) & (HEAD - 1)) < half
        cos_t = jnp.concatenate([cos_p] * reps, axis=1) if reps > 1 else cos_p
        sin_t = jnp.concatenate([sin_p] * reps, axis=1) if reps > 1 else sin_p
        rot = jnp.where(first_half, -pltpu.roll(xn, w - half, 1), pltpu.roll(xn, half, 1))
        return xn * cos_t + rot * sin_t

    q = pb[:,0:sw_w]
    qn = q * lax.rsqrt(_head_sums(q * q) * (1.0 / HEAD) + RMS_EPS) * (qg_ref[...] * (HEAD ** -0.5))
    qr = rope(qn).astype(BF16)
    kx = pb[:,sw_w:sw_w + kv_w]
    kn = kx * lax.rsqrt(_head_sums(kx * kx) * (1.0 / HEAD) + RMS_EPS) * kg_ref[...]
    kr = rope(kn)
    vx = pb[:,sw_w + kv_w:sw_w + 2 * kv_w]

    lo_mask = lax.broadcasted_iota(jnp.int32, (1, LANES), 1) < HEAD
    qi = lax.broadcasted_iota(jnp.int32, (SW_BLOCK, 2 * SW_BLOCK), 0)
    kj = lax.broadcasted_iota(jnp.int32, (SW_BLOCK, 2 * SW_BLOCK), 1)
    allowed = (kj > qi) & (kj <= qi + SW_BLOCK)
    n_sub = tm // SW_BLOCK
    n_kv = kv_w // HEAD
    n_pairs = sw_w // LANES
    pairs_per_kv = n_pairs // n_kv

    v_bds, masks, s_b = [], [], []
    for sb in range(n_sub):
        rs = slice(sb * SW_BLOCK, (sb + 1) * SW_BLOCK)
        if sb == 0:
            kband = jnp.concatenate([kprev_ref[...], kr[rs]], axis=0)
            vband = jnp.concatenate([vprev_ref[...], vx[rs]], axis=0)
            masks.append(allowed & (kj >= jnp.where(first, SW_BLOCK, 0)))
        else:
            kband = kr[(sb - 1) * SW_BLOCK:(sb + 1) * SW_BLOCK]
            vband = vx[(sb - 1) * SW_BLOCK:(sb + 1) * SW_BLOCK]
            masks.append(allowed)
        kband_sw = pltpu.roll(kband, HEAD, 1)
        vband_sw = pltpu.roll(vband, HEAD, 1)
        for g in range(n_kv):
            k_lo = kband if g == 0 else kband_sw
            k_hi = kband_sw if g == 0 else kband
            v_lo = vband if g == 0 else vband_sw
            v_hi = vband_sw if g == 0 else vband
            k_bd = jnp.concatenate([jnp.where(lo_mask, k_lo, 0.0), jnp.where(lo_mask, 0.0, k_hi)],
                                   axis=0).astype(BF16)
            v_bds.append(jnp.concatenate([jnp.where(lo_mask, v_lo, 0.0), jnp.where(lo_mask, 0.0, v_hi)],
                                         axis=0).astype(BF16))
            for pp in range(pairs_per_kv):
                p = g * pairs_per_kv + pp
                s_b.append(_dot_nt(qr[rs, p * LANES:(p + 1) * LANES], k_bd))
    kprev_ref[...] = kr[tm - SW_BLOCK:, :]
    vprev_ref[...] = vx[tm - SW_BLOCK:, :]

    pg_bc = _dot(h, win_ref[:, g_0 + d:g_0 + 3 * d])
    proj_a = _dot(ya_ref[...].astype(BF16), wa_ref[...])

    p_c = []
    for j in range(n_xh):
        m = jnp.max(s_c[j], axis=-1, keepdims=True)
        e = jnp.exp(s_c[j] - m)
        p_c.append((e / jnp.sum(e, axis=-1, keepdims=True)).astype(BF16))
    p_b = []
    for sb in range(n_sub):
        for p in range(n_pairs):
            s2 = s_b[sb * n_pairs + p]
            probs = []
            for hh in range(2):
                s = jnp.where(masks[sb], s2[:, hh * 2 * SW_BLOCK:(hh + 1) * 2 * SW_BLOCK], NEG_INF)
                sink = sink_ref[0, 2 * p + hh]
                m = jnp.maximum(jnp.max(s, axis=-1, keepdims=True), sink)
                e = jnp.exp(s - m)
                den = jnp.sum(e, axis=-1, keepdims=True) + jnp.exp(sink - m)
                probs.append((e / den).astype(BF16))
            p_b.append(jnp.concatenate(probs, axis=1))
    o_c = [_dot(p_c[j], vm_ref[:, j * x_hd:(j + 1) * x_hd].astype(BF16)) for j in range(n_xh)]
    o_b = [[_dot(p_b[sb * n_pairs + p], v_bds[sb * n_kv + p // pairs_per_kv]) for p in range(n_pairs)]
           for sb in range(n_sub)]

    y_c = (jnp.concatenate(o_c, axis=1) * _silu(pc[:,x_w:2 * x_w])).astype(BF16)
    proj_c = _dot(y_c, wc_ref[...])
    y_b = jnp.concatenate([jnp.concatenate(o_b[sb], axis=1) for sb in range(n_sub)], axis=0)
    y_b = (y_b * _silu(pb[:,sw_w + 2 * kv_w:2 * sw_w + 2 * kv_w])).astype(BF16)
    proj_b = _dot(y_b, wb_ref[...])
    merged = _sigmoid(pg_a) * proj_a
    merged = merged + _sigmoid(pg_bc[:, d:2 * d]) * proj_c
    merged = merged + _sigmoid(pg_bc[:, 0:d]) * proj_b
    o_ref[...] = x_ref[...] + _dot(merged.astype(BF16), wo_ref[...])


def _attn_out(x2, ya, pos2, km, vm, params, bsz, t, m_len, sw_w, kv_w, x_w, x_hd):
    n, d = x2.shape
    nt = t // ATTN_TILE
    row = lambda a: pl.BlockSpec((ATTN_TILE, a.shape[1]), lambda b, i: (b * nt + i, 0))
    full = lambda a: pl.BlockSpec(a.shape, lambda b, i: (0,) * a.ndim, pipeline_mode=pl.Buffered(1))
    memspec = pl.BlockSpec((m_len, x_w), lambda b, i: (b, 0))
    in_specs = [row(x2), row(ya), row(pos2), memspec, memspec]
    for a in params:
        in_specs.append(full(a))
    in_specs[5 + 5] = pl.BlockSpec(memory_space=pltpu.SMEM)
    return pl.pallas_call(
        functools.partial(_attn_out_kernel, sw_w=sw_w, kv_w=kv_w, x_w=x_w, x_hd=x_hd),
        grid=(bsz, nt),
        in_specs=in_specs,
        out_specs=pl.BlockSpec((ATTN_TILE, d), lambda b, i: (b * nt + i, 0)),
        out_shape=jax.ShapeDtypeStruct((n, d), F32),
        scratch_shapes=[
            pltpu.VMEM((SW_BLOCK, kv_w), F32),
            pltpu.VMEM((SW_BLOCK, kv_w), F32),
        ],
        compiler_params=pltpu.CompilerParams(
            dimension_semantics=("arbitrary", "arbitrary"), vmem_limit_bytes=VMEM_LIMIT),
        name="attn_out",
    )(x2, ya, pos2, km, vm, *params)


def kernel(x, mem, positions, norm_g, mem_norm_g, w_in, mu_rkv, mu_wa, w0, w2, a0, a2, k_k, k_a, r_k,
           lnx_g, lnx_b, q_norm_g, k_norm_g, sinks, xq_norm_g, xk_norm_g, w_mem_kv,
           w_proj_a, w_proj_b, w_proj_c, w_out):
    bsz, t, d = x.shape
    m_len = mem.shape[1]
    depth = w_in.shape[0]
    rw_w = w0.shape[1]
    lora = w2.shape[1]
    sw_w = w_proj_b.shape[1]
    x_w = w_proj_c.shape[1]
    x_hd = xq_norm_g.shape[1]
    kv_w = (w_in.shape[2] - (4 * rw_w + 2 * lora) - 2 * sw_w - 2 * x_w - 3 * d) // 2
    assert t % ROW_TILE == 0 and t % ATTN_TILE == 0 and rw_w % LANES == 0 and sw_w % LANES == 0 and kv_w == LANES
    assert q_norm_g.shape[1] == HEAD and r_k.shape[2] == HEAD and 2 * lora == LANES
    n = bsz * t
    x2 = x.reshape(n, d)
    mem2 = mem.reshape(bsz * m_len, d)
    pos2 = positions.reshape(n, 1)
    half = HEAD // 2
    inv = ROPE_THETA ** (-(jnp.arange(LANES) % half).astype(F32) / half)
    inv = inv.reshape(1, LANES)
    for l in range(depth):
        row = lambda a: a[l].reshape(1, -1)
        w_in_bf = w_in[l].astype(BF16)
        km, vm = _mem_kv(mem2, row(mem_norm_g), w_mem_kv[l].astype(BF16), row(xk_norm_g), m_len, x_w, x_hd)
        mu = jnp.concatenate([mu_rkv[l].reshape(1, -1), mu_wa[l].reshape(1, -1)], axis=1)
        zeros = jnp.zeros((lora, rw_w), F32)
        lora_w = jnp.concatenate([jnp.concatenate([w2[l], zeros], axis=1),
                                  jnp.concatenate([zeros, a2[l]], axis=1)], axis=0)
        lora_hi = lora_w.astype(BF16)
        lora_lo = (lora_w - lora_hi.astype(F32)).astype(BF16)
        t_i = np.arange(bsz * CHUNK)
        tri = jnp.asarray((t_i[:, None] >= t_i[None, :]) & (t_i[:, None] // CHUNK == t_i[None, :] // CHUNK),
                          dtype=BF16)
        rw_params = (row(norm_g), w_in_bf, tri, mu, row(w0), row(a0), lora_hi, lora_lo, row(k_k), row(k_a), row(r_k),
                     row(lnx_g), row(lnx_b))
        ya = _rwkv(x2.reshape(bsz, t, d), rw_w, 2 * lora, rw_params)
        at_params = (row(norm_g), w_in_bf, inv,
                     jnp.tile(row(q_norm_g), (1, sw_w // HEAD)), jnp.tile(row(k_norm_g), (1, kv_w // HEAD)),
                     row(sinks), row(xq_norm_g),
                     w_proj_a[l].astype(BF16), w_proj_b[l].astype(BF16), w_proj_c[l].astype(BF16),
                     w_out[l].astype(BF16))
        x2 = _attn_out(x2, ya, pos2, km, vm, at_params, bsz, t, m_len, sw_w, kv_w, x_w, x_hd)
    return x2.reshape(bsz, t, d)
```

```python
import functools
import math

import jax
import jax.numpy as jnp
import numpy as np
from jax import lax
from jax.experimental import pallas as pl
from jax.experimental.pallas import tpu as pltpu

F32 = jnp.float32
BF16 = jnp.bfloat16

RMS_EPS = 1e-6
LNX_EPS = 64e-5
L2_EPS = 1e-12
DECAY_SCALE = math.exp(-0.5)
ROPE_THETA = 10000.0
NEG_INF = -1e30

HEAD = 64
LANES = 128
CHUNK = 64
SW_BLOCK = 128
ROW_TILE = 256
ATTN_TILE = 512
VMEM_LIMIT = 56 * 1024 * 1024


def _dot(a, b):
    return jnp.dot(a, b, preferred_element_type=F32)


def _dot_nt(a, b):
    return lax.dot_general(a, b, (((1,), (1,)), ((), ())), preferred_element_type=F32)


def _dot_tn(a, b):
    return lax.dot_general(a, b, (((0,), (0,)), ((), ())), preferred_element_type=F32)


def _split2(x):
    hi = x.astype(BF16)
    lo = (x - hi.astype(F32)).astype(BF16)
    return hi, lo


def _split3(x):
    hi = x.astype(BF16)
    r1 = x - hi.astype(F32)
    mid = r1.astype(BF16)
    lo = (r1 - mid.astype(F32)).astype(BF16)
    return hi, mid, lo


def _dot_x3(a, b_hi, b_lo):
    a_hi, a_lo = _split2(a)
    return _dot(a_hi, b_hi) + (_dot(a_hi, b_lo) + _dot(a_lo, b_hi))


def _head_sums(x):
    lo_mask = lax.broadcasted_iota(jnp.int32, (1, LANES), 1) < HEAD
    outs = []
    for p in range(x.shape[1] // LANES):
        xs = x[:, p * LANES:(p + 1) * LANES]
        s_lo = jnp.sum(jnp.where(lo_mask, xs, 0.0), axis=-1, keepdims=True)
        s_hi = jnp.sum(jnp.where(lo_mask, 0.0, xs), axis=-1, keepdims=True)
        outs.append(jnp.where(lo_mask, s_lo, s_hi))
    return outs[0] if len(outs) == 1 else jnp.concatenate(outs, axis=1)


def _sigmoid(x):
    return 0.5 * jnp.tanh(0.5 * x) + 0.5


def _silu(x):
    return x * _sigmoid(x)


def _block_diag2(y, lo_mask):
    return jnp.concatenate([jnp.where(lo_mask, y, 0.0), jnp.where(lo_mask, 0.0, y)], axis=0)


def _mem_kv_kernel(mem_ref, g_ref, w_ref, kg_ref, k_ref, v_ref, *, xw, hd):
    m = mem_ref[...]
    ms = jnp.mean(m * m, axis=-1, keepdims=True)
    h = (m * lax.rsqrt(ms + RMS_EPS) * g_ref[...]).astype(BF16)
    kv = _dot(h, w_ref[...])
    v_ref[...] = kv[:, xw:]
    for j in range(xw // hd):
        kj = kv[:, j * hd:(j + 1) * hd]
        msk = jnp.mean(kj * kj, axis=-1, keepdims=True)
        k_ref[:, j * hd:(j + 1) * hd] = kj * lax.rsqrt(msk + RMS_EPS) * kg_ref[...]


def _mem_kv(mem2, g, w_bf, kg, m_len, xw, hd):
    n, d = mem2.shape
    return pl.pallas_call(
        functools.partial(_mem_kv_kernel, xw=xw, hd=hd),
        grid=(n // m_len,),
        in_specs=[
            pl.BlockSpec((m_len, d), lambda i: (i, 0)),
            pl.BlockSpec((1, d), lambda i: (0, 0)),
            pl.BlockSpec(w_bf.shape, lambda i: (0, 0)),
            pl.BlockSpec((1, hd), lambda i: (0, 0)),
        ],
        out_specs=[pl.BlockSpec((m_len, xw), lambda i: (i, 0))] * 2,
        out_shape=[jax.ShapeDtypeStruct((n, xw), F32)] * 2,
        compiler_params=pltpu.CompilerParams(
            dimension_semantics=("arbitrary",), vmem_limit_bytes=VMEM_LIMIT),
        name="mem_kv",
    )(mem2, g, w_bf, kg)


def _rwkv_kernel(x0_ref, x_ref, ng_ref, win_ref, tri_ref, mu_ref, w0_ref, a0_ref, lora_hi_ref, lora_lo_ref, kk_ref,
                 ka_ref, rk_ref, lng_ref, lnb_ref, ya_ref, pa_ref, pan_ref, carry_ref, s_ref, *, width, lora2):
    nb, tt, _ = x_ref.shape
    n_pairs = width // LANES
    mixw = 3 * width + lora2
    nrow = nb * CHUNK

    def project(src_ref, dst_ref):
        for b in range(nb):
            xin = src_ref[b]
            h = (xin * lax.rsqrt(jnp.mean(xin * xin, axis=-1, keepdims=True) + RMS_EPS)
                 * ng_ref[...]).astype(BF16)
            dst_ref[b] = _dot(h, win_ref[:, 0:mixw + width])

    @pl.when(pl.program_id(0) == 0)
    def _():
        carry_ref[...] = jnp.zeros_like(carry_ref)
        s_ref[...] = jnp.zeros_like(s_ref)
        project(x0_ref, pa_ref)

    ti = lax.broadcasted_iota(jnp.int32, (CHUNK, LANES), 0)
    li = lax.broadcasted_iota(jnp.int32, (CHUNK, LANES), 1)
    lo_mask = li < HEAD
    lj = jnp.where(lo_mask, li, li - HEAD)
    strict = ti > lj
    incl = ti >= lj
    eye = (ti == lj).astype(F32)
    tri = tri_ref[...]
    row_id = lax.broadcasted_iota(jnp.int32, (nrow, 1), 0)
    mu = mu_ref[...]
    lane_l = lax.broadcasted_iota(jnp.int32, (1, lora2), 1)

    def bd(y):
        return _block_diag2(y, lo_mask).astype(BF16)

    def prep(c):
        r0 = c * CHUNK
        u = jnp.concatenate([pa_ref[b, r0:r0 + CHUNK, 0:mixw] for b in range(nb)], axis=0)
        prev = pltpu.roll(u, 1, 0)
        for b in range(nb):
            before = carry_ref[b:b + 1, :] if c == 0 else pa_ref[b, r0 - 1:r0, 0:mixw]
            prev = jnp.where(row_id == b * CHUNK, before, prev)
        mixed = u + (prev - u) * mu
        r = mixed[:, 0:width]
        k = mixed[:, width:2 * width]
        v = mixed[:, 2 * width:3 * width]
        lo_ra = mixed[:, 3 * width:mixw]
        lo_in = jnp.where(lane_l < lora2 // 2, jnp.tanh(lo_ra), lo_ra)
        proj = _dot_x3(lo_in, lora_hi_ref[...], lora_lo_ref[...])
        lw = -DECAY_SCALE * _sigmoid(w0_ref[...] + proj[:, 0:width])
        a_sig = _sigmoid(a0_ref[...] + proj[:, width:2 * width])
        kk = k * kk_ref[...]
        kkn = kk * lax.rsqrt(jnp.maximum(_head_sums(kk * kk), L2_EPS * L2_EPS))
        k2 = k * (1.0 + (a_sig - 1.0) * ka_ref[...])
        bvec = kkn * a_sig
        l_hi, l_mid, l_lo = _split3(lw)
        cs = _dot(tri, l_hi) + (_dot(tri, l_mid) + _dot(tri, l_lo))
        return dict(r0=r0, r=r, v=v, lw=lw, kkn=kkn, k2=k2, bvec=bvec, cs=cs)

    def prep_tail(q):
        r0, r, v, lw, kkn, k2, bvec, cs = (q[n] for n in ("r0", "r", "v", "lw", "kkn", "k2", "bvec", "cs"))
        cs_last = [cs[(b + 1) * CHUNK - 1:(b + 1) * CHUNK, :] for b in range(nb)]
        cs_end = jnp.concatenate([jnp.broadcast_to(cl, (CHUNK, width)) for cl in cs_last], axis=0)
        w_last = [jnp.exp(cl) for cl in cs_last]
        e_in = jnp.exp(cs)
        e_neg = jnp.exp(-cs)
        e_tail = jnp.exp(cs_end - cs)
        gate = jnp.concatenate([_silu(pa_ref[b, r0:r0 + CHUNK, mixw:mixw + width]) for b in range(nb)], axis=0)
        return dict(r=r, k2=k2, v=v, w_last=w_last, gate=gate, rt=r * e_in, at=-kkn * jnp.exp(cs - lw),
                    kt=k2 * e_neg, bt=bvec * e_neg, kh=k2 * e_tail, bh=bvec * e_tail)

    n_ch = tt // CHUNK
    pre_head = [prep(c) for c in range(n_ch)]

    project(x_ref, pan_ref)

    pre = [prep_tail(q) for q in pre_head]
    for b in range(nb):
        carry_ref[b:b + 1, :] = pa_ref[b, tt - 1:tt, 0:mixw]

    def blk(c, name, b, p):
        return pre[c][name][b * CHUNK:(b + 1) * CHUNK, p * LANES:(p + 1) * LANES]

    n_bp = nb * n_pairs
    ares = {}

    def phase_a(chunks):
        probs = [(c, b, p) for c in chunks for b in range(nb) for p in range(n_pairs)]
        idx = range(len(probs))
        lhs2 = [jnp.concatenate([blk(c, "at", b, p), blk(c, "rt", b, p)], axis=0).astype(BF16)
                for c, b, p in probs]
        g = [_dot_nt(lhs2[i], jnp.concatenate([bd(blk(c, "bt", b, p)), bd(blk(c, "kt", b, p))], axis=0))
             for i, (c, b, p) in enumerate(probs)]
        yield
        a_ab = [jnp.where(strict, g[i][0:CHUNK, 0:LANES], 0.0) for i in idx]
        a_ak = [jnp.where(strict, g[i][0:CHUNK, LANES:], 0.0) for i in idx]
        a_r = [jnp.concatenate([jnp.where(incl, g[i][CHUNK:, 0:LANES], 0.0),
                                jnp.where(incl, g[i][CHUNK:, LANES:], 0.0)], axis=1).astype(BF16) for i in idx]
        v_bd = [bd(blk(c, "v", b, p)) for c, b, p in probs]
        av = [_dot(a_ak[i].astype(BF16), v_bd[i]) for i in idx]
        yield
        x = [eye + a_ab[i] for i in idx]
        pw = [_dot(a_ab[i].astype(BF16), bd(a_ab[i])) for i in idx]
        yield
        for _ in range(4):
            px = [_dot(jnp.concatenate([pw[i], x[i]], axis=0).astype(BF16), bd(pw[i])) for i in idx]
            pw = [px[i][0:CHUNK] for i in idx]
            x = [x[i] + px[i][CHUNK:] for i in idx]
            yield
        for i, prob in enumerate(probs):
            ares[prob] = dict(lhs2=lhs2[i], a_r=a_r[i], v_bd=v_bd[i], av=av[i],
                              x=(x[i] + _dot(x[i].astype(BF16), bd(pw[i]))).astype(BF16))
        yield

    state = [s_ref[j] for j in range(n_bp)]

    def phase_b(c):
        r0 = c * CHUNK
        bp = [(b, p) for b in range(nb) for p in range(n_pairs)]
        res = [ares[(c, b, p)] for b, p in bp]
        asrs = [_dot_nt(res[j]["lhs2"], bd(state[j])) for j in range(n_bp)]
        yield
        z = [asrs[j][0:CHUNK] + res[j]["av"] for j in range(n_bp)]
        u_p = [_dot(res[j]["x"], bd(z[j])) for j in range(n_bp)]
        yield
        d = [_dot_tn(jnp.concatenate([u_p[j], blk(c, "v", b, p)], axis=0).astype(BF16),
                     jnp.concatenate([blk(c, "bh", b, p), blk(c, "kh", b, p)], axis=0).astype(BF16))
             for j, (b, p) in enumerate(bp)]
        for j, (b, p) in enumerate(bp):
            state[j] = (state[j] * pre[c]["w_last"][b][:, p * LANES:(p + 1) * LANES]
                        + jnp.where(lo_mask, d[j][0:HEAD], d[j][HEAD:]))
        yield
        y = [asrs[j][CHUNK:] + _dot(res[j]["a_r"], jnp.concatenate([bd(u_p[j]), res[j]["v_bd"]], axis=0))
             for j in range(n_bp)]
        y_all = jnp.concatenate(
            [jnp.concatenate([y[b * n_pairs + p] for p in range(n_pairs)], axis=1) for b in range(nb)], axis=0)
        inv_n = 1.0 / HEAD
        yc = y_all - _head_sums(y_all) * inv_n
        var = _head_sums(yc * yc) * inv_n
        yn = yc * lax.rsqrt(var + LNX_EPS) * lng_ref[...] + lnb_ref[...]
        bonus = _head_sums(pre[c]["r"] * pre[c]["k2"] * rk_ref[...]) * pre[c]["v"]
        for b in range(nb):
            bs = slice(b * CHUNK, (b + 1) * CHUNK)
            ya_ref[b, r0:r0 + CHUNK, :] = (yn[bs] + bonus[bs]) * pre[c]["gate"][bs]
        yield

    half_ch = n_ch // 2
    for _ in phase_a(range(half_ch)):
        pass

    def recurrence(chunks):
        for c in chunks:
            yield from phase_b(c)

    b_gen = recurrence(range(half_ch))
    for _ in phase_a(range(half_ch, n_ch)):
        next(b_gen, None)
    for _ in b_gen:
        pass
    for _ in recurrence(range(half_ch, n_ch)):
        pass
    for j in range(n_bp):
        s_ref[j] = state[j]
    pa_ref[...] = pan_ref[...]


def _rwkv(x3, width, lora2, params):
    bsz, t, d = x3.shape
    nt = t // ROW_TILE
    a_w = 4 * width + lora2
    full = lambda a: pl.BlockSpec(a.shape, lambda i: (0,) * a.ndim, pipeline_mode=pl.Buffered(1))
    ya = pl.pallas_call(
        functools.partial(_rwkv_kernel, width=width, lora2=lora2),
        grid=(nt,),
        in_specs=[pl.BlockSpec((bsz, ROW_TILE, d), lambda i: (0, 0, 0), pipeline_mode=pl.Buffered(1)),
                  pl.BlockSpec((bsz, ROW_TILE, d), lambda i: (0, jnp.minimum(i + 1, nt - 1), 0))]
        + [full(a) for a in params],
        out_specs=pl.BlockSpec((bsz, ROW_TILE, width), lambda i: (0, i, 0)),
        out_shape=jax.ShapeDtypeStruct((bsz, t, width), F32),
        scratch_shapes=[
            pltpu.VMEM((bsz, ROW_TILE, a_w), F32),
            pltpu.VMEM((bsz, ROW_TILE, a_w), F32),
            pltpu.VMEM((bsz, 3 * width + lora2), F32),
            pltpu.VMEM((bsz * (width // LANES), HEAD, LANES), F32),
        ],
        compiler_params=pltpu.CompilerParams(
            dimension_semantics=("arbitrary",), vmem_limit_bytes=VMEM_LIMIT),
        name="rwkv",
    )(x3, x3, *params)
    return ya.reshape(bsz * t, width)


def _attn_out_kernel(x_ref, ya_ref, pos_ref, km_ref, vm_ref, ng_ref, win_ref,
                     inv_ref, qg_ref, kg_ref, sink_ref, xqg_ref,
                     wa_ref, wb_ref, wc_ref, wo_ref, o_ref, kprev_ref, vprev_ref,
                     *, sw_w, kv_w, x_w, x_hd):
    tm = x_ref.shape[0]
    d = x_ref.shape[1]
    first = pl.program_id(1) == 0

    @pl.when(first)
    def _():
        kprev_ref[...] = jnp.zeros_like(kprev_ref)
        vprev_ref[...] = jnp.zeros_like(vprev_ref)


    xin = x_ref[...]
    h = (xin * lax.rsqrt(jnp.mean(xin * xin, axis=-1, keepdims=True) + RMS_EPS) * ng_ref[...]).astype(BF16)
    b_w = 2 * sw_w + 2 * kv_w
    c_w = 2 * x_w
    b_0 = win_ref.shape[1] - (b_w + c_w + 3 * d)
    c_0 = b_0 + b_w
    g_0 = c_0 + c_w
    pc = _dot(h, win_ref[:, c_0:g_0])
    n_xh = x_w // x_hd
    xqn = []
    for j in range(n_xh):
        xq = pc[:,j * x_hd:(j + 1) * x_hd]
        xqn.append((xq * lax.rsqrt(jnp.mean(xq * xq, axis=-1, keepdims=True) + RMS_EPS)
                    * (xqg_ref[...] * (x_hd ** -0.5))).astype(BF16))
    s_c = [_dot_nt(xqn[j], km_ref[:, j * x_hd:(j + 1) * x_hd].astype(BF16)) for j in range(n_xh)]
    pb = _dot(h, win_ref[:, b_0:c_0])
    pg_a = _dot(h, win_ref[:, g_0:g_0 + d])

    half = HEAD // 2
    n_grp = LANES // half
    qrows = tm // n_grp
    pos = pos_ref[...].astype(F32)
    lane_grp = lax.broadcasted_iota(jnp.int32, (1, LANES), 1) // half
    pos_d = pos[0:qrows]
    for j in range(1, n_grp):
        pos_d = jnp.where(lane_grp == j, pos[j * qrows:(j + 1) * qrows], pos_d)
    ang_d = pos_d * inv_ref[...]
    cos_d = jnp.cos(ang_d)
    sin_d = jnp.sin(ang_d)

    def spread(tab):
        rolled = [tab] + [pltpu.roll(tab, half * k, 1) for k in range(1, n_grp)]
        quarters = []
        for j in range(n_grp):
            cj = rolled[(0 - j) % n_grp]
            for gl in range(1, n_grp):
                cj = jnp.where(lane_grp == gl, rolled[(gl - j) % n_grp], cj)
            quarters.append(cj)
        return jnp.concatenate(quarters, axis=0)

    cos_p = spread(cos_d)
    sin_p = spread(sin_d)

    def rope(xn):
        w = xn.shape[1]
        reps = w // LANES
        first_half = (lax.broadcasted_iota(jnp.int32, (1, w), 1) & (HEAD - 1)) < half
        cos_t = jnp.concatenate([cos_p] * reps, axis=1) if reps > 1 else cos_p
        sin_t = jnp.concatenate([sin_p] * reps, axis=1) if reps > 1 else sin_p
        rot = jnp.where(first_half, -pltpu.roll(xn, w - half, 1), pltpu.roll(xn, half, 1))
        return xn * cos_t + rot * sin_t

    q = pb[:,0:sw_w]
    qn = q * lax.rsqrt(_head_sums(q * q) * (1.0 / HEAD) + RMS_EPS) * (qg_ref[...] * (HEAD ** -0.5))
    qr = rope(qn).astype(BF16)
    kx = pb[:,sw_w:sw_w + kv_w]
    kn = kx * lax.rsqrt(_head_sums(kx * kx) * (1.0 / HEAD) + RMS_EPS) * kg_ref[...]
    kr = rope(kn)
    vx = pb[:,sw_w + kv_w:sw_w + 2 * kv_w]

    lo_mask = lax.broadcasted_iota(jnp.int32, (1, LANES), 1) < HEAD
    qi = lax.broadcasted_iota(jnp.int32, (SW_BLOCK, 2 * SW_BLOCK), 0)
    kj = lax.broadcasted_iota(jnp.int32, (SW_BLOCK, 2 * SW_BLOCK), 1)
    allowed = (kj > qi) & (kj <= qi + SW_BLOCK)
    n_sub = tm // SW_BLOCK
    n_kv = kv_w // HEAD
    n_pairs = sw_w // LANES
    pairs_per_kv = n_pairs // n_kv

    v_bds, masks, s_b = [], [], []
    for sb in range(n_sub):
        rs = slice(sb * SW_BLOCK, (sb + 1) * SW_BLOCK)
        if sb == 0:
            kband = jnp.concatenate([kprev_ref[...], kr[rs]], axis=0)
            vband = jnp.concatenate([vprev_ref[...], vx[rs]], axis=0)
            masks.append(allowed & (kj >= jnp.where(first, SW_BLOCK, 0)))
        else:
            kband = kr[(sb - 1) * SW_BLOCK:(sb + 1) * SW_BLOCK]
            vband = vx[(sb - 1) * SW_BLOCK:(sb + 1) * SW_BLOCK]
            masks.append(allowed)
        kband_sw = pltpu.roll(kband, HEAD, 1)
        vband_sw = pltpu.roll(vband, HEAD, 1)
        for g in range(n_kv):
            k_lo = kband if g == 0 else kband_sw
            k_hi = kband_sw if g == 0 else kband
            v_lo = vband if g == 0 else vband_sw
            v_hi = vband_sw if g == 0 else vband
            k_bd = jnp.concatenate([jnp.where(lo_mask, k_lo, 0.0), jnp.where(lo_mask, 0.0, k_hi)],
                                   axis=0).astype(BF16)
            v_bds.append(jnp.concatenate([jnp.where(lo_mask, v_lo, 0.0), jnp.where(lo_mask, 0.0, v_hi)],
                                         axis=0).astype(BF16))
            for pp in range(pairs_per_kv):
                p = g * pairs_per_kv + pp
                s_b.append(_dot_nt(qr[rs, p * LANES:(p + 1) * LANES], k_bd))
    kprev_ref[...] = kr[tm - SW_BLOCK:, :]
    vprev_ref[...] = vx[tm - SW_BLOCK:, :]

    pg_bc = _dot(h, win_ref[:, g_0 + d:g_0 + 3 * d])
    proj_a = _dot(ya_ref[...].astype(BF16), wa_ref[...])

    p_c = []
    for j in range(n_xh):
        m = jnp.max(s_c[j], axis=-1, keepdims=True)
        e = jnp.exp(s_c[j] - m)
        p_c.append((e / jnp.sum(e, axis=-1, keepdims=True)).astype(BF16))
    p_b = []
    for sb in range(n_sub):
        for p in range(n_pairs):
            s2 = s_b[sb * n_pairs + p]
            probs = []
            for hh in range(2):
                s = jnp.where(masks[sb], s2[:, hh * 2 * SW_BLOCK:(hh + 1) * 2 * SW_BLOCK], NEG_INF)
                sink = sink_ref[0, 2 * p + hh]
                m = jnp.maximum(jnp.max(s, axis=-1, keepdims=True), sink)
                e = jnp.exp(s - m)
                den = jnp.sum(e, axis=-1, keepdims=True) + jnp.exp(sink - m)
                probs.append((e / den).astype(BF16))
            p_b.append(jnp.concatenate(probs, axis=1))
    o_c = [_dot(p_c[j], vm_ref[:, j * x_hd:(j + 1) * x_hd].astype(BF16)) for j in range(n_xh)]
    o_b = [[_dot(p_b[sb * n_pairs + p], v_bds[sb * n_kv + p // pairs_per_kv]) for p in range(n_pairs)]
           for sb in range(n_sub)]

    y_c = (jnp.concatenate(o_c, axis=1) * _silu(pc[:,x_w:2 * x_w])).astype(BF16)
    proj_c = _dot(y_c, wc_ref[...])
    y_b = jnp.concatenate([jnp.concatenate(o_b[sb], axis=1) for sb in range(n_sub)], axis=0)
    y_b = (y_b * _silu(pb[:,sw_w + 2 * kv_w:2 * sw_w + 2 * kv_w])).astype(BF16)
    proj_b = _dot(y_b, wb_ref[...])
    merged = _sigmoid(pg_a) * proj_a
    merged = merged + _sigmoid(pg_bc[:, d:2 * d]) * proj_c
    merged = merged + _sigmoid(pg_bc[:, 0:d]) * proj_b
    o_ref[...] = x_ref[...] + _dot(merged.astype(BF16), wo_ref[...])


def _attn_out(x2, ya, pos2, km, vm, params, bsz, t, m_len, sw_w, kv_w, x_w, x_hd):
    n, d = x2.shape
    nt = t // ATTN_TILE
    row = lambda a: pl.BlockSpec((ATTN_TILE, a.shape[1]), lambda b, i: (b * nt + i, 0))
    full = lambda a: pl.BlockSpec(a.shape, lambda b, i: (0,) * a.ndim, pipeline_mode=pl.Buffered(1))
    memspec = pl.BlockSpec((m_len, x_w), lambda b, i: (b, 0))
    in_specs = [row(x2), row(ya), row(pos2), memspec, memspec]
    for a in params:
        in_specs.append(full(a))
    in_specs[5 + 5] = pl.BlockSpec(memory_space=pltpu.SMEM)
    return pl.pallas_call(
        functools.partial(_attn_out_kernel, sw_w=sw_w, kv_w=kv_w, x_w=x_w, x_hd=x_hd),
        grid=(bsz, nt),
        in_specs=in_specs,
        out_specs=pl.BlockSpec((ATTN_TILE, d), lambda b, i: (b * nt + i, 0)),
        out_shape=jax.ShapeDtypeStruct((n, d), F32),
        scratch_shapes=[
            pltpu.VMEM((SW_BLOCK, kv_w), F32),
            pltpu.VMEM((SW_BLOCK, kv_w), F32),
        ],
        compiler_params=pltpu.CompilerParams(
            dimension_semantics=("arbitrary", "arbitrary"), vmem_limit_bytes=VMEM_LIMIT),
        name="attn_out",
    )(x2, ya, pos2, km, vm, *params)


def kernel(x, mem, positions, norm_g, mem_norm_g, w_in, mu_rkv, mu_wa, w0, w2, a0, a2, k_k, k_a, r_k,
           lnx_g, lnx_b, q_norm_g, k_norm_g, sinks, xq_norm_g, xk_norm_g, w_mem_kv,
           w_proj_a, w_proj_b, w_proj_c, w_out):
    bsz, t, d = x.shape
    m_len = mem.shape[1]
    depth = w_in.shape[0]
    rw_w = w0.shape[1]
    lora = w2.shape[1]
    sw_w = w_proj_b.shape[1]
    x_w = w_proj_c.shape[1]
    x_hd = xq_norm_g.shape[1]
    kv_w = (w_in.shape[2] - (4 * rw_w + 2 * lora) - 2 * sw_w - 2 * x_w - 3 * d) // 2
    assert t % ROW_TILE == 0 and t % ATTN_TILE == 0 and rw_w % LANES == 0 and sw_w % LANES == 0 and kv_w == LANES
    assert q_norm_g.shape[1] == HEAD and r_k.shape[2] == HEAD and 2 * lora == LANES
    n = bsz * t
    x2 = x.reshape(n, d)
    mem2 = mem.reshape(bsz * m_len, d)
    pos2 = positions.reshape(n, 1)
    half = HEAD // 2
    inv = ROPE_THETA ** (-(jnp.arange(LANES) % half).astype(F32) / half)
    inv = inv.reshape(1, LANES)
    for l in range(depth):
        row = lambda a: a[l].reshape(1, -1)
        w_in_bf = w_in[l].astype(BF16)
        km, vm = _mem_kv(mem2, row(mem_norm_g), w_mem_kv[l].astype(BF16), row(xk_norm_g), m_len, x_w, x_hd)
        mu = jnp.concatenate([mu_rkv[l].reshape(1, -1), mu_wa[l].reshape(1, -1)], axis=1)
        zeros = jnp.zeros((lora, rw_w), F32)
        lora_w = jnp.concatenate([jnp.concatenate([w2[l], zeros], axis=1),
                                  jnp.concatenate([zeros, a2[l]], axis=1)], axis=0)
        lora_hi = lora_w.astype(BF16)
        lora_lo = (lora_w - lora_hi.astype(F32)).astype(BF16)
        t_i = np.arange(bsz * CHUNK)
        tri = jnp.asarray((t_i[:, None] >= t_i[None, :]) & (t_i[:, None] // CHUNK == t_i[None, :] // CHUNK),
                          dtype=BF16)
        rw_params = (row(norm_g), w_in_bf, tri, mu, row(w0), row(a0), lora_hi, lora_lo, row(k_k), row(k_a), row(r_k),
                     row(lnx_g), row(lnx_b))
        ya = _rwkv(x2.reshape(bsz, t, d), rw_w, 2 * lora, rw_params)
        at_params = (row(norm_g), w_in_bf, inv,
                     jnp.tile(row(q_norm_g), (1, sw_w // HEAD)), jnp.tile(row(k_norm_g), (1, kv_w // HEAD)),
                     row(sinks), row(xq_norm_g),
                     w_proj_a[l].astype(BF16), w_proj_b[l].astype(BF16), w_proj_c[l].astype(BF16),
                     w_out[l].astype(BF16))
        x2 = _attn_out(x2, ya, pos2, km, vm, at_params, bsz, t, m_len, sw_w, kv_w, x_w, x_hd)
    return x2.reshape(bsz, t, d)
```

```python
import functools
import math

import jax
import jax.numpy as jnp
import numpy as np
from jax import lax
from jax.experimental import pallas as pl
from jax.experimental.pallas import tpu as pltpu

F32 = jnp.float32
BF16 = jnp.bfloat16

RMS_EPS = 1e-6
LNX_EPS = 64e-5
L2_EPS = 1e-12
DECAY_SCALE = math.exp(-0.5)
ROPE_THETA = 10000.0
NEG_INF = -1e30

HEAD = 64
LANES = 128
CHUNK = 64
SW_BLOCK = 128
ROW_TILE = 256
ATTN_TILE = 512
VMEM_LIMIT = 56 * 1024 * 1024


def _dot(a, b):
    return jnp.dot(a, b, preferred_element_type=F32)


def _dot_nt(a, b):
    return lax.dot_general(a, b, (((1,), (1,)), ((), ())), preferred_element_type=F32)


def _dot_tn(a, b):
    return lax.dot_general(a, b, (((0,), (0,)), ((), ())), preferred_element_type=F32)


def _split2(x):
    hi = x.astype(BF16)
    lo = (x - hi.astype(F32)).astype(BF16)
    return hi, lo


def _split3(x):
    hi = x.astype(BF16)
    r1 = x - hi.astype(F32)
    mid = r1.astype(BF16)
    lo = (r1 - mid.astype(F32)).astype(BF16)
    return hi, mid, lo


def _dot_x3(a, b_hi, b_lo):
    a_hi, a_lo = _split2(a)
    return _dot(a_hi, b_hi) + (_dot(a_hi, b_lo) + _dot(a_lo, b_hi))


def _head_sums(x):
    lo_mask = lax.broadcasted_iota(jnp.int32, (1, LANES), 1) < HEAD
    outs = []
    for p in range(x.shape[1] // LANES):
        xs = x[:, p * LANES:(p + 1) * LANES]
        s_lo = jnp.sum(jnp.where(lo_mask, xs, 0.0), axis=-1, keepdims=True)
        s_hi = jnp.sum(jnp.where(lo_mask, 0.0, xs), axis=-1, keepdims=True)
        outs.append(jnp.where(lo_mask, s_lo, s_hi))
    return outs[0] if len(outs) == 1 else jnp.concatenate(outs, axis=1)


def _sigmoid(x):
    return 0.5 * jnp.tanh(0.5 * x) + 0.5


def _silu(x):
    return x * _sigmoid(x)


def _block_diag2(y, lo_mask):
    return jnp.concatenate([jnp.where(lo_mask, y, 0.0), jnp.where(lo_mask, 0.0, y)], axis=0)


def _mem_kv_kernel(mem_ref, g_ref, w_ref, kg_ref, k_ref, v_ref, *, xw, hd):
    m = mem_ref[...]
    ms = jnp.mean(m * m, axis=-1, keepdims=True)
    h = (m * lax.rsqrt(ms + RMS_EPS) * g_ref[...]).astype(BF16)
    kv = _dot(h, w_ref[...])
    v_ref[...] = kv[:, xw:].astype(v_ref.dtype)
    for j in range(xw // hd):
        kj = kv[:, j * hd:(j + 1) * hd]
        msk = jnp.mean(kj * kj, axis=-1, keepdims=True)
        k_ref[:, j * hd:(j + 1) * hd] = (kj * lax.rsqrt(msk + RMS_EPS) * kg_ref[...]).astype(k_ref.dtype)


def _mem_kv(mem2, g, w_bf, kg, m_len, xw, hd):
    n, d = mem2.shape
    return pl.pallas_call(
        functools.partial(_mem_kv_kernel, xw=xw, hd=hd),
        grid=(n // m_len,),
        in_specs=[
            pl.BlockSpec((m_len, d), lambda i: (i, 0)),
            pl.BlockSpec((1, d), lambda i: (0, 0)),
            pl.BlockSpec(w_bf.shape, lambda i: (0, 0)),
            pl.BlockSpec((1, hd), lambda i: (0, 0)),
        ],
        out_specs=[pl.BlockSpec((m_len, xw), lambda i: (i, 0))] * 2,
        out_shape=[jax.ShapeDtypeStruct((n, xw), BF16)] * 2,
        compiler_params=pltpu.CompilerParams(
            dimension_semantics=("arbitrary",), vmem_limit_bytes=VMEM_LIMIT),
        name="mem_kv",
    )(mem2, g, w_bf, kg)


def _rwkv_kernel(x0_ref, x_ref, ng_ref, win_ref, tri_ref, mu_ref, w0_ref, a0_ref, lora_hi_ref, lora_lo_ref, kk_ref,
                 ka_ref, rk_ref, lng_ref, lnb_ref, ya_ref, pa_ref, pan_ref, carry_ref, s_ref, *, width, lora2):
    nb, tt, _ = x_ref.shape
    n_pairs = width // LANES
    mixw = 3 * width + lora2
    nrow = nb * CHUNK

    def project(src_ref, dst_ref):
        for b in range(nb):
            xin = src_ref[b]
            h = (xin * lax.rsqrt(jnp.mean(xin * xin, axis=-1, keepdims=True) + RMS_EPS)
                 * ng_ref[...]).astype(BF16)
            dst_ref[b] = _dot(h, win_ref[:, 0:mixw + width])

    @pl.when(pl.program_id(0) == 0)
    def _():
        carry_ref[...] = jnp.zeros_like(carry_ref)
        s_ref[...] = jnp.zeros_like(s_ref)
        project(x0_ref, pa_ref)

    ti = lax.broadcasted_iota(jnp.int32, (CHUNK, LANES), 0)
    li = lax.broadcasted_iota(jnp.int32, (CHUNK, LANES), 1)
    lo_mask = li < HEAD
    lj = jnp.where(lo_mask, li, li - HEAD)
    strict = ti > lj
    incl = ti >= lj
    eye = (ti == lj).astype(F32)
    tri = tri_ref[...]
    row_id = lax.broadcasted_iota(jnp.int32, (nrow, 1), 0)
    mu = mu_ref[...]
    lane_l = lax.broadcasted_iota(jnp.int32, (1, lora2), 1)

    def bd(y):
        return _block_diag2(y, lo_mask).astype(BF16)

    def prep(c):
        r0 = c * CHUNK
        u = jnp.concatenate([pa_ref[b, r0:r0 + CHUNK, 0:mixw] for b in range(nb)], axis=0)
        prev = pltpu.roll(u, 1, 0)
        for b in range(nb):
            before = carry_ref[b:b + 1, :] if c == 0 else pa_ref[b, r0 - 1:r0, 0:mixw]
            prev = jnp.where(row_id == b * CHUNK, before, prev)
        mixed = u + (prev - u) * mu
        r = mixed[:, 0:width]
        k = mixed[:, width:2 * width]
        v = mixed[:, 2 * width:3 * width]
        lo_ra = mixed[:, 3 * width:mixw]
        lo_in = jnp.where(lane_l < lora2 // 2, jnp.tanh(lo_ra), lo_ra)
        proj = _dot_x3(lo_in, lora_hi_ref[...], lora_lo_ref[...])
        lw = -DECAY_SCALE * _sigmoid(w0_ref[...] + proj[:, 0:width])
        a_sig = _sigmoid(a0_ref[...] + proj[:, width:2 * width])
        kk = k * kk_ref[...]
        kkn = kk * lax.rsqrt(jnp.maximum(_head_sums(kk * kk), L2_EPS * L2_EPS))
        k2 = k * (1.0 + (a_sig - 1.0) * ka_ref[...])
        bvec = kkn * a_sig
        l_hi, l_mid, l_lo = _split3(lw)
        cs = _dot(tri, l_hi) + (_dot(tri, l_mid) + _dot(tri, l_lo))
        return dict(r0=r0, r=r, v=v, lw=lw, kkn=kkn, k2=k2, bvec=bvec, cs=cs)

    def prep_tail(q):
        r0, r, v, lw, kkn, k2, bvec, cs = (q[n] for n in ("r0", "r", "v", "lw", "kkn", "k2", "bvec", "cs"))
        cs_last = [cs[(b + 1) * CHUNK - 1:(b + 1) * CHUNK, :] for b in range(nb)]
        cs_end = jnp.concatenate([jnp.broadcast_to(cl, (CHUNK, width)) for cl in cs_last], axis=0)
        w_last = [jnp.exp(cl) for cl in cs_last]
        e_in = jnp.exp(cs)
        e_neg = jnp.exp(-cs)
        e_tail = jnp.exp(cs_end - cs)
        gate = jnp.concatenate([_silu(pa_ref[b, r0:r0 + CHUNK, mixw:mixw + width]) for b in range(nb)], axis=0)
        return dict(r=r, k2=k2, v=v, w_last=w_last, gate=gate, rt=r * e_in, at=-kkn * jnp.exp(cs - lw),
                    kt=k2 * e_neg, bt=bvec * e_neg, kh=k2 * e_tail, bh=bvec * e_tail)

    n_ch = tt // CHUNK
    pre_head = [prep(c) for c in range(n_ch)]

    project(x_ref, pan_ref)

    pre = [prep_tail(q) for q in pre_head]
    for b in range(nb):
        carry_ref[b:b + 1, :] = pa_ref[b, tt - 1:tt, 0:mixw]

    def blk(c, name, b, p):
        return pre[c][name][b * CHUNK:(b + 1) * CHUNK, p * LANES:(p + 1) * LANES]

    n_bp = nb * n_pairs
    ares = {}

    def phase_a(chunks):
        probs = [(c, b, p) for c in chunks for b in range(nb) for p in range(n_pairs)]
        idx = range(len(probs))
        lhs2 = [jnp.concatenate([blk(c, "at", b, p), blk(c, "rt", b, p)], axis=0).astype(BF16)
                for c, b, p in probs]
        g = [_dot_nt(lhs2[i], jnp.concatenate([bd(blk(c, "bt", b, p)), bd(blk(c, "kt", b, p))], axis=0))
             for i, (c, b, p) in enumerate(probs)]
        yield
        a_ab = [jnp.where(strict, g[i][0:CHUNK, 0:LANES], 0.0) for i in idx]
        a_ak = [jnp.where(strict, g[i][0:CHUNK, LANES:], 0.0) for i in idx]
        a_r = [jnp.concatenate([jnp.where(incl, g[i][CHUNK:, 0:LANES], 0.0),
                                jnp.where(incl, g[i][CHUNK:, LANES:], 0.0)], axis=1).astype(BF16) for i in idx]
        v_bd = [bd(blk(c, "v", b, p)) for c, b, p in probs]
        av = [_dot(a_ak[i].astype(BF16), v_bd[i]) for i in idx]
        yield
        x = [eye + a_ab[i] for i in idx]
        pw = [_dot(a_ab[i].astype(BF16), bd(a_ab[i])) for i in idx]
        yield
        for _ in range(4):
            px = [_dot(jnp.concatenate([pw[i], x[i]], axis=0).astype(BF16), bd(pw[i])) for i in idx]
            pw = [px[i][0:CHUNK] for i in idx]
            x = [x[i] + px[i][CHUNK:] for i in idx]
            yield
        for i, prob in enumerate(probs):
            ares[prob] = dict(lhs2=lhs2[i], a_r=a_r[i], v_bd=v_bd[i], av=av[i],
                              x=(x[i] + _dot(x[i].astype(BF16), bd(pw[i]))).astype(BF16))
        yield

    state = [s_ref[j] for j in range(n_bp)]

    def phase_b(c):
        r0 = c * CHUNK
        bp = [(b, p) for b in range(nb) for p in range(n_pairs)]
        res = [ares[(c, b, p)] for b, p in bp]
        asrs = [_dot_nt(res[j]["lhs2"], bd(state[j])) for j in range(n_bp)]
        yield
        z = [asrs[j][0:CHUNK] + res[j]["av"] for j in range(n_bp)]
        u_p = [_dot(res[j]["x"], bd(z[j])) for j in range(n_bp)]
        yield
        d = [_dot_tn(jnp.concatenate([u_p[j], blk(c, "v", b, p)], axis=0).astype(BF16),
                     jnp.concatenate([blk(c, "bh", b, p), blk(c, "kh", b, p)], axis=0).astype(BF16))
             for j, (b, p) in enumerate(bp)]
        for j, (b, p) in enumerate(bp):
            state[j] = (state[j] * pre[c]["w_last"][b][:, p * LANES:(p + 1) * LANES]
                        + jnp.where(lo_mask, d[j][0:HEAD], d[j][HEAD:]))
        yield
        y = [asrs[j][CHUNK:] + _dot(res[j]["a_r"], jnp.concatenate([bd(u_p[j]), res[j]["v_bd"]], axis=0))
             for j in range(n_bp)]
        y_all = jnp.concatenate(
            [jnp.concatenate([y[b * n_pairs + p] for p in range(n_pairs)], axis=1) for b in range(nb)], axis=0)
        inv_n = 1.0 / HEAD
        yc = y_all - _head_sums(y_all) * inv_n
        var = _head_sums(yc * yc) * inv_n
        yn = yc * lax.rsqrt(var + LNX_EPS) * lng_ref[...] + lnb_ref[...]
        bonus = _head_sums(pre[c]["r"] * pre[c]["k2"] * rk_ref[...]) * pre[c]["v"]
        for b in range(nb):
            bs = slice(b * CHUNK, (b + 1) * CHUNK)
            ya_ref[b, r0:r0 + CHUNK, :] = ((yn[bs] + bonus[bs]) * pre[c]["gate"][bs]).astype(ya_ref.dtype)
        yield

    half_ch = n_ch // 2
    for _ in phase_a(range(half_ch)):
        pass

    def recurrence(chunks):
        for c in chunks:
            yield from phase_b(c)

    b_gen = recurrence(range(half_ch))
    for _ in phase_a(range(half_ch, n_ch)):
        next(b_gen, None)
    for _ in b_gen:
        pass
    for _ in recurrence(range(half_ch, n_ch)):
        pass
    for j in range(n_bp):
        s_ref[j] = state[j]
    pa_ref[...] = pan_ref[...]


def _rwkv(x3, width, lora2, params):
    bsz, t, d = x3.shape
    nt = t // ROW_TILE
    a_w = 4 * width + lora2
    full = lambda a: pl.BlockSpec(a.shape, lambda i: (0,) * a.ndim, pipeline_mode=pl.Buffered(1))
    ya = pl.pallas_call(
        functools.partial(_rwkv_kernel, width=width, lora2=lora2),
        grid=(nt,),
        in_specs=[pl.BlockSpec((bsz, ROW_TILE, d), lambda i: (0, 0, 0), pipeline_mode=pl.Buffered(1)),
                  pl.BlockSpec((bsz, ROW_TILE, d), lambda i: (0, jnp.minimum(i + 1, nt - 1), 0))]
        + [full(a) for a in params],
        out_specs=pl.BlockSpec((bsz, ROW_TILE, width), lambda i: (0, i, 0)),
        out_shape=jax.ShapeDtypeStruct((bsz, t, width), BF16),
        scratch_shapes=[
            pltpu.VMEM((bsz, ROW_TILE, a_w), F32),
            pltpu.VMEM((bsz, ROW_TILE, a_w), F32),
            pltpu.VMEM((bsz, 3 * width + lora2), F32),
            pltpu.VMEM((bsz * (width // LANES), HEAD, LANES), F32),
        ],
        compiler_params=pltpu.CompilerParams(
            dimension_semantics=("arbitrary",), vmem_limit_bytes=VMEM_LIMIT),
        name="rwkv",
    )(x3, x3, *params)
    return ya.reshape(bsz * t, width)


def _attn_out_kernel(x_ref, ya_ref, pos_ref, km_ref, vm_ref, ng_ref, win_ref,
                     inv_ref, qg_ref, kg_ref, sink_ref, xqg_ref,
                     wa_ref, wb_ref, wc_ref, wo_ref, o_ref, kprev_ref, vprev_ref,
                     *, sw_w, kv_w, x_w, x_hd):
    tm = x_ref.shape[0]
    d = x_ref.shape[1]
    first = pl.program_id(1) == 0

    @pl.when(first)
    def _():
        kprev_ref[...] = jnp.zeros_like(kprev_ref)
        vprev_ref[...] = jnp.zeros_like(vprev_ref)


    xin = x_ref[...]
    h = (xin * lax.rsqrt(jnp.mean(xin * xin, axis=-1, keepdims=True) + RMS_EPS) * ng_ref[...]).astype(BF16)
    b_w = 2 * sw_w + 2 * kv_w
    c_w = 2 * x_w
    b_0 = win_ref.shape[1] - (b_w + c_w + 3 * d)
    c_0 = b_0 + b_w
    g_0 = c_0 + c_w
    pc = _dot(h, win_ref[:, c_0:g_0])
    n_xh = x_w // x_hd
    xqn = []
    for j in range(n_xh):
        xq = pc[:,j * x_hd:(j + 1) * x_hd]
        xqn.append((xq * lax.rsqrt(jnp.mean(xq * xq, axis=-1, keepdims=True) + RMS_EPS)
                    * (xqg_ref[...] * (x_hd ** -0.5))).astype(BF16))
    s_c = [_dot_nt(xqn[j], km_ref[:, j * x_hd:(j + 1) * x_hd]) for j in range(n_xh)]
    pb = _dot(h, win_ref[:, b_0:c_0])
    pg_a = _dot(h, win_ref[:, g_0:g_0 + d])

    half = HEAD // 2
    n_grp = LANES // half
    qrows = tm // n_grp
    pos = pos_ref[...].astype(F32)
    lane_grp = lax.broadcasted_iota(jnp.int32, (1, LANES), 1) // half
    pos_d = pos[0:qrows]
    for j in range(1, n_grp):
        pos_d = jnp.where(lane_grp == j, pos[j * qrows:(j + 1) * qrows], pos_d)
    ang_d = pos_d * inv_ref[...]
    cos_d = jnp.cos(ang_d)
    sin_d = jnp.sin(ang_d)

    def spread(tab):
        rolled = [tab] + [pltpu.roll(tab, half * k, 1) for k in range(1, n_grp)]
        quarters = []
        for j in range(n_grp):
            cj = rolled[(0 - j) % n_grp]
            for gl in range(1, n_grp):
                cj = jnp.where(lane_grp == gl, rolled[(gl - j) % n_grp], cj)
            quarters.append(cj)
        return jnp.concatenate(quarters, axis=0)

    cos_p = spread(cos_d)
    sin_p = spread(sin_d)

    def rope(xn):
        w = xn.shape[1]
        reps = w // LANES
        first_half = (lax.broadcasted_iota(jnp.int32, (1, w), 1) & (HEAD - 1)) < half
        cos_t = jnp.concatenate([cos_p] * reps, axis=1) if reps > 1 else cos_p
        sin_t = jnp.concatenate([sin_p] * reps, axis=1) if reps > 1 else sin_p
        rot = jnp.where(first_half, -pltpu.roll(xn, w - half, 1), pltpu.roll(xn, half, 1))
        return xn * cos_t + rot * sin_t

    q = pb[:,0:sw_w]
    qn = q * lax.rsqrt(_head_sums(q * q) * (1.0 / HEAD) + RMS_EPS) * (qg_ref[...] * (HEAD ** -0.5))
    qr = rope(qn).astype(BF16)
    kx = pb[:,sw_w:sw_w + kv_w]
    kn = kx * lax.rsqrt(_head_sums(kx * kx) * (1.0 / HEAD) + RMS_EPS) * kg_ref[...]
    kr = rope(kn)
    vx = pb[:,sw_w + kv_w:sw_w + 2 * kv_w]

    lo_mask = lax.broadcasted_iota(jnp.int32, (1, LANES), 1) < HEAD
    qi = lax.broadcasted_iota(jnp.int32, (SW_BLOCK, 2 * SW_BLOCK), 0)
    kj = lax.broadcasted_iota(jnp.int32, (SW_BLOCK, 2 * SW_BLOCK), 1)
    allowed = (kj > qi) & (kj <= qi + SW_BLOCK)
    n_sub = tm // SW_BLOCK
    n_kv = kv_w // HEAD
    n_pairs = sw_w // LANES
    pairs_per_kv = n_pairs // n_kv

    v_bds, masks, s_b = [], [], []
    for sb in range(n_sub):
        rs = slice(sb * SW_BLOCK, (sb + 1) * SW_BLOCK)
        if sb == 0:
            kband = jnp.concatenate([kprev_ref[...], kr[rs]], axis=0)
            vband = jnp.concatenate([vprev_ref[...], vx[rs]], axis=0)
            masks.append(allowed & (kj >= jnp.where(first, SW_BLOCK, 0)))
        else:
            kband = kr[(sb - 1) * SW_BLOCK:(sb + 1) * SW_BLOCK]
            vband = vx[(sb - 1) * SW_BLOCK:(sb + 1) * SW_BLOCK]
            masks.append(allowed)
        kband_sw = pltpu.roll(kband, HEAD, 1)
        vband_sw = pltpu.roll(vband, HEAD, 1)
        for g in range(n_kv):
            k_lo = kband if g == 0 else kband_sw
            k_hi = kband_sw if g == 0 else kband
            v_lo = vband if g == 0 else vband_sw
            v_hi = vband_sw if g == 0 else vband
            k_bd = jnp.concatenate([jnp.where(lo_mask, k_lo, 0.0), jnp.where(lo_mask, 0.0, k_hi)],
                                   axis=0).astype(BF16)
            v_bds.append(jnp.concatenate([jnp.where(lo_mask, v_lo, 0.0), jnp.where(lo_mask, 0.0, v_hi)],
                                         axis=0).astype(BF16))
            for pp in range(pairs_per_kv):
                p = g * pairs_per_kv + pp
                s_b.append(_dot_nt(qr[rs, p * LANES:(p + 1) * LANES], k_bd))
    kprev_ref[...] = kr[tm - SW_BLOCK:, :]
    vprev_ref[...] = vx[tm - SW_BLOCK:, :]

    pg_bc = _dot(h, win_ref[:, g_0 + d:g_0 + 3 * d])
    proj_a = _dot(ya_ref[...], wa_ref[...])

    p_c = []
    for j in range(n_xh):
        m = jnp.max(s_c[j], axis=-1, keepdims=True)
        e = jnp.exp(s_c[j] - m)
        p_c.append((e / jnp.sum(e, axis=-1, keepdims=True)).astype(BF16))
    p_b = []
    for sb in range(n_sub):
        for p in range(n_pairs):
            s2 = s_b[sb * n_pairs + p]
            probs = []
            for hh in range(2):
                s = jnp.where(masks[sb], s2[:, hh * 2 * SW_BLOCK:(hh + 1) * 2 * SW_BLOCK], NEG_INF)
                sink = sink_ref[0, 2 * p + hh]
                m = jnp.maximum(jnp.max(s, axis=-1, keepdims=True), sink)
                e = jnp.exp(s - m)
                den = jnp.sum(e, axis=-1, keepdims=True) + jnp.exp(sink - m)
                probs.append((e / den).astype(BF16))
            p_b.append(jnp.concatenate(probs, axis=1))
    o_c = [_dot(p_c[j], vm_ref[:, j * x_hd:(j + 1) * x_hd]) for j in range(n_xh)]
    o_b = [[_dot(p_b[sb * n_pairs + p], v_bds[sb * n_kv + p // pairs_per_kv]) for p in range(n_pairs)]
           for sb in range(n_sub)]

    y_c = (jnp.concatenate(o_c, axis=1) * _silu(pc[:,x_w:2 * x_w])).astype(BF16)
    proj_c = _dot(y_c, wc_ref[...])
    y_b = jnp.concatenate([jnp.concatenate(o_b[sb], axis=1) for sb in range(n_sub)], axis=0)
    y_b = (y_b * _silu(pb[:,sw_w + 2 * kv_w:2 * sw_w + 2 * kv_w])).astype(BF16)
    proj_b = _dot(y_b, wb_ref[...])
    merged = _sigmoid(pg_a) * proj_a
    merged = merged + _sigmoid(pg_bc[:, d:2 * d]) * proj_c
    merged = merged + _sigmoid(pg_bc[:, 0:d]) * proj_b
    o_ref[...] = x_ref[...] + _dot(merged.astype(BF16), wo_ref[...])


def _attn_out(x2, ya, pos2, km, vm, params, bsz, t, m_len, sw_w, kv_w, x_w, x_hd):
    n, d = x2.shape
    nt = t // ATTN_TILE
    row = lambda a: pl.BlockSpec((ATTN_TILE, a.shape[1]), lambda b, i: (b * nt + i, 0))
    full = lambda a: pl.BlockSpec(a.shape, lambda b, i: (0,) * a.ndim, pipeline_mode=pl.Buffered(1))
    memspec = pl.BlockSpec((m_len, x_w), lambda b, i: (b, 0))
    in_specs = [row(x2), row(ya), row(pos2), memspec, memspec]
    for a in params:
        in_specs.append(full(a))
    in_specs[5 + 5] = pl.BlockSpec(memory_space=pltpu.SMEM)
    return pl.pallas_call(
        functools.partial(_attn_out_kernel, sw_w=sw_w, kv_w=kv_w, x_w=x_w, x_hd=x_hd),
        grid=(bsz, nt),
        in_specs=in_specs,
        out_specs=pl.BlockSpec((ATTN_TILE, d), lambda b, i: (b * nt + i, 0)),
        out_shape=jax.ShapeDtypeStruct((n, d), F32),
        scratch_shapes=[
            pltpu.VMEM((SW_BLOCK, kv_w), F32),
            pltpu.VMEM((SW_BLOCK, kv_w), F32),
        ],
        compiler_params=pltpu.CompilerParams(
            dimension_semantics=("arbitrary", "arbitrary"), vmem_limit_bytes=VMEM_LIMIT),
        name="attn_out",
    )(x2, ya, pos2, km, vm, *params)


def kernel(x, mem, positions, norm_g, mem_norm_g, w_in, mu_rkv, mu_wa, w0, w2, a0, a2, k_k, k_a, r_k,
           lnx_g, lnx_b, q_norm_g, k_norm_g, sinks, xq_norm_g, xk_norm_g, w_mem_kv,
           w_proj_a, w_proj_b, w_proj_c, w_out):
    bsz, t, d = x.shape
    m_len = mem.shape[1]
    depth = w_in.shape[0]
    rw_w = w0.shape[1]
    lora = w2.shape[1]
    sw_w = w_proj_b.shape[1]
    x_w = w_proj_c.shape[1]
    x_hd = xq_norm_g.shape[1]
    kv_w = (w_in.shape[2] - (4 * rw_w + 2 * lora) - 2 * sw_w - 2 * x_w - 3 * d) // 2
    assert t % ROW_TILE == 0 and t % ATTN_TILE == 0 and rw_w % LANES == 0 and sw_w % LANES == 0 and kv_w == LANES
    assert q_norm_g.shape[1] == HEAD and r_k.shape[2] == HEAD and 2 * lora == LANES
    n = bsz * t
    x2 = x.reshape(n, d)
    mem2 = mem.reshape(bsz * m_len, d)
    pos2 = positions.reshape(n, 1)
    half = HEAD // 2
    inv = ROPE_THETA ** (-(jnp.arange(LANES) % half).astype(F32) / half)
    inv = inv.reshape(1, LANES)
    for l in range(depth):
        row = lambda a: a[l].reshape(1, -1)
        w_in_bf = w_in[l].astype(BF16)
        km, vm = _mem_kv(mem2, row(mem_norm_g), w_mem_kv[l].astype(BF16), row(xk_norm_g), m_len, x_w, x_hd)
        mu = jnp.concatenate([mu_rkv[l].reshape(1, -1), mu_wa[l].reshape(1, -1)], axis=1)
        zeros = jnp.zeros((lora, rw_w), F32)
        lora_w = jnp.concatenate([jnp.concatenate([w2[l], zeros], axis=1),
                                  jnp.concatenate([zeros, a2[l]], axis=1)], axis=0)
        lora_hi = lora_w.astype(BF16)
        lora_lo = (lora_w - lora_hi.astype(F32)).astype(BF16)
        t_i = np.arange(bsz * CHUNK)
        tri = jnp.asarray((t_i[:, None] >= t_i[None, :]) & (t_i[:, None] // CHUNK == t_i[None, :] // CHUNK),
                          dtype=BF16)
        rw_params = (row(norm_g), w_in_bf, tri, mu, row(w0), row(a0), lora_hi, lora_lo, row(k_k), row(k_a), row(r_k),
                     row(lnx_g), row(lnx_b))
        ya = _rwkv(x2.reshape(bsz, t, d), rw_w, 2 * lora, rw_params)
        at_params = (row(norm_g), w_in_bf, inv,
                     jnp.tile(row(q_norm_g), (1, sw_w // HEAD)), jnp.tile(row(k_norm_g), (1, kv_w // HEAD)),
                     row(sinks), row(xq_norm_g),
                     w_proj_a[l].astype(BF16), w_proj_b[l].astype(BF16), w_proj_c[l].astype(BF16),
                     w_out[l].astype(BF16))
        x2 = _attn_out(x2, ya, pos2, km, vm, at_params, bsz, t, m_len, sw_w, kv_w, x_w, x_hd)
    return x2.reshape(bsz, t, d)
```

```python
import functools
import math

import jax
import jax.numpy as jnp
import numpy as np
from jax import lax
from jax.experimental import pallas as pl
from jax.experimental.pallas import tpu as pltpu

F32 = jnp.float32
BF16 = jnp.bfloat16

RMS_EPS = 1e-6
LNX_EPS = 64e-5
L2_EPS = 1e-12
DECAY_SCALE = math.exp(-0.5)
ROPE_THETA = 10000.0
NEG_INF = -1e30

HEAD = 64
LANES = 128
CHUNK = 64
SW_BLOCK = 128
ROW_TILE = 256
ATTN_TILE = 512
VMEM_LIMIT = 56 * 1024 * 1024


def _dot(a, b):
    return jnp.dot(a, b, preferred_element_type=F32)


def _dot_nt(a, b):
    return lax.dot_general(a, b, (((1,), (1,)), ((), ())), preferred_element_type=F32)


def _dot_tn(a, b):
    return lax.dot_general(a, b, (((0,), (0,)), ((), ())), preferred_element_type=F32)


def _split2(x):
    hi = x.astype(BF16)
    lo = (x - hi.astype(F32)).astype(BF16)
    return hi, lo


def _split3(x):
    hi = x.astype(BF16)
    r1 = x - hi.astype(F32)
    mid = r1.astype(BF16)
    lo = (r1 - mid.astype(F32)).astype(BF16)
    return hi, mid, lo


def _dot_x3(a, b_hi, b_lo):
    a_hi, a_lo = _split2(a)
    return _dot(a_hi, b_hi) + (_dot(a_hi, b_lo) + _dot(a_lo, b_hi))


def _head_sums(x):
    lo_mask = lax.broadcasted_iota(jnp.int32, (1, LANES), 1) < HEAD
    outs = []
    for p in range(x.shape[1] // LANES):
        xs = x[:, p * LANES:(p + 1) * LANES]
        s_lo = jnp.sum(jnp.where(lo_mask, xs, 0.0), axis=-1, keepdims=True)
        s_hi = jnp.sum(jnp.where(lo_mask, 0.0, xs), axis=-1, keepdims=True)
        outs.append(jnp.where(lo_mask, s_lo, s_hi))
    return outs[0] if len(outs) == 1 else jnp.concatenate(outs, axis=1)


def _sigmoid(x):
    return 0.5 * jnp.tanh(0.5 * x) + 0.5


def _silu(x):
    return x * _sigmoid(x)


def _block_diag2(y, lo_mask):
    return jnp.concatenate([jnp.where(lo_mask, y, 0.0), jnp.where(lo_mask, 0.0, y)], axis=0)


def _mem_kv_kernel(mem_ref, g_ref, w_ref, kg_ref, k_ref, v_ref, *, xw, hd):
    m = mem_ref[...]
    ms = jnp.mean(m * m, axis=-1, keepdims=True)
    h = (m * lax.rsqrt(ms + RMS_EPS) * g_ref[...]).astype(BF16)
    kv = _dot(h, w_ref[...])
    v_ref[...] = kv[:, xw:].astype(v_ref.dtype)
    for j in range(xw // hd):
        kj = kv[:, j * hd:(j + 1) * hd]
        msk = jnp.mean(kj * kj, axis=-1, keepdims=True)
        k_ref[:, j * hd:(j + 1) * hd] = (kj * lax.rsqrt(msk + RMS_EPS) * kg_ref[...]).astype(k_ref.dtype)


def _mem_kv(mem2, g, w_bf, kg, m_len, xw, hd):
    n, d = mem2.shape
    return pl.pallas_call(
        functools.partial(_mem_kv_kernel, xw=xw, hd=hd),
        grid=(n // m_len,),
        in_specs=[
            pl.BlockSpec((m_len, d), lambda i: (i, 0)),
            pl.BlockSpec((1, d), lambda i: (0, 0)),
            pl.BlockSpec(w_bf.shape, lambda i: (0, 0)),
            pl.BlockSpec((1, hd), lambda i: (0, 0)),
        ],
        out_specs=[pl.BlockSpec((m_len, xw), lambda i: (i, 0))] * 2,
        out_shape=[jax.ShapeDtypeStruct((n, xw), BF16)] * 2,
        compiler_params=pltpu.CompilerParams(
            dimension_semantics=("arbitrary",), vmem_limit_bytes=VMEM_LIMIT),
        name="mem_kv",
    )(mem2, g, w_bf, kg)


def _rwkv_kernel(x0_ref, x_ref, ng_ref, win_ref, tri_ref, mu_ref, w0_ref, a0_ref, lora_hi_ref, lora_lo_ref, kk_ref,
                 ka_ref, rk_ref, lng_ref, lnb_ref, ya_ref, pa_ref, pan_ref, carry_ref, s_ref, *, width, lora2):
    nb, tt, _ = x_ref.shape
    n_pairs = width // LANES
    mixw = 3 * width + lora2
    nrow = nb * CHUNK

    def project(src_ref, dst_ref):
        for b in range(nb):
            xin = src_ref[b]
            h = (xin * lax.rsqrt(jnp.mean(xin * xin, axis=-1, keepdims=True) + RMS_EPS)
                 * ng_ref[...]).astype(BF16)
            dst_ref[b] = _dot(h, win_ref[:, 0:mixw + width])

    @pl.when(pl.program_id(0) == 0)
    def _():
        carry_ref[...] = jnp.zeros_like(carry_ref)
        s_ref[...] = jnp.zeros_like(s_ref)
        project(x0_ref, pa_ref)

    ti = lax.broadcasted_iota(jnp.int32, (CHUNK, LANES), 0)
    li = lax.broadcasted_iota(jnp.int32, (CHUNK, LANES), 1)
    lo_mask = li < HEAD
    lj = jnp.where(lo_mask, li, li - HEAD)
    strict = ti > lj
    incl = ti >= lj
    eye = (ti == lj).astype(F32)
    tri = tri_ref[...]
    row_id = lax.broadcasted_iota(jnp.int32, (nrow, 1), 0)
    mu = mu_ref[...]
    lane_l = lax.broadcasted_iota(jnp.int32, (1, lora2), 1)

    def bd(y):
        return _block_diag2(y, lo_mask).astype(BF16)

    def prep(c):
        r0 = c * CHUNK
        u = jnp.concatenate([pa_ref[b, r0:r0 + CHUNK, 0:mixw] for b in range(nb)], axis=0)
        prev = pltpu.roll(u, 1, 0)
        for b in range(nb):
            before = carry_ref[b:b + 1, :] if c == 0 else pa_ref[b, r0 - 1:r0, 0:mixw]
            prev = jnp.where(row_id == b * CHUNK, before, prev)
        mixed = u + (prev - u) * mu
        r = mixed[:, 0:width]
        k = mixed[:, width:2 * width]
        v = mixed[:, 2 * width:3 * width]
        lo_ra = mixed[:, 3 * width:mixw]
        lo_in = jnp.where(lane_l < lora2 // 2, jnp.tanh(lo_ra), lo_ra)
        proj = _dot_x3(lo_in, lora_hi_ref[...], lora_lo_ref[...])
        lw = -DECAY_SCALE * _sigmoid(w0_ref[...] + proj[:, 0:width])
        a_sig = _sigmoid(a0_ref[...] + proj[:, width:2 * width])
        kk = k * kk_ref[...]
        kkn = kk * lax.rsqrt(jnp.maximum(_head_sums(kk * kk), L2_EPS * L2_EPS))
        k2 = k * (1.0 + (a_sig - 1.0) * ka_ref[...])
        bvec = kkn * a_sig
        l_hi, l_mid, l_lo = _split3(lw)
        cs = _dot(tri, l_hi) + (_dot(tri, l_mid) + _dot(tri, l_lo))
        return dict(r0=r0, r=r, v=v, lw=lw, kkn=kkn, k2=k2, bvec=bvec, cs=cs)

    def prep_tail(q):
        r0, r, v, lw, kkn, k2, bvec, cs = (q[n] for n in ("r0", "r", "v", "lw", "kkn", "k2", "bvec", "cs"))
        cs_last = [cs[(b + 1) * CHUNK - 1:(b + 1) * CHUNK, :] for b in range(nb)]
        cs_end = jnp.concatenate([jnp.broadcast_to(cl, (CHUNK, width)) for cl in cs_last], axis=0)
        w_last = [jnp.exp(cl) for cl in cs_last]
        e_in = jnp.exp(cs)
        e_neg = jnp.exp(-cs)
        e_tail = jnp.exp(cs_end - cs)
        gate = jnp.concatenate([_silu(pa_ref[b, r0:r0 + CHUNK, mixw:mixw + width]) for b in range(nb)], axis=0)
        return dict(r=r, k2=k2, v=v, w_last=w_last, gate=gate, rt=r * e_in, at=-kkn * jnp.exp(cs - lw),
                    kt=k2 * e_neg, bt=bvec * e_neg, kh=k2 * e_tail, bh=bvec * e_tail)

    n_ch = tt // CHUNK
    pre_head = [prep(c) for c in range(n_ch)]

    pre = [prep_tail(q) for q in pre_head]
    for b in range(nb):
        carry_ref[b:b + 1, :] = pa_ref[b, tt - 1:tt, 0:mixw]

    def blk(c, name, b, p):
        return pre[c][name][b * CHUNK:(b + 1) * CHUNK, p * LANES:(p + 1) * LANES]

    n_bp = nb * n_pairs
    ares = {}

    def phase_a(chunks):
        probs = [(c, b, p) for c in chunks for b in range(nb) for p in range(n_pairs)]
        idx = range(len(probs))
        lhs2 = [jnp.concatenate([blk(c, "at", b, p), blk(c, "rt", b, p)], axis=0).astype(BF16)
                for c, b, p in probs]
        g = [_dot_nt(lhs2[i], jnp.concatenate([bd(blk(c, "bt", b, p)), bd(blk(c, "kt", b, p))], axis=0))
             for i, (c, b, p) in enumerate(probs)]
        yield
        a_ab = [jnp.where(strict, g[i][0:CHUNK, 0:LANES], 0.0) for i in idx]
        a_ak = [jnp.where(strict, g[i][0:CHUNK, LANES:], 0.0) for i in idx]
        a_r = [jnp.concatenate([jnp.where(incl, g[i][CHUNK:, 0:LANES], 0.0),
                                jnp.where(incl, g[i][CHUNK:, LANES:], 0.0)], axis=1).astype(BF16) for i in idx]
        v_bd = [bd(blk(c, "v", b, p)) for c, b, p in probs]
        av = [_dot(a_ak[i].astype(BF16), v_bd[i]) for i in idx]
        yield
        x = [eye + a_ab[i] for i in idx]
        pw = [_dot(a_ab[i].astype(BF16), bd(a_ab[i])) for i in idx]
        yield
        for _ in range(4):
            px = [_dot(jnp.concatenate([pw[i], x[i]], axis=0).astype(BF16), bd(pw[i])) for i in idx]
            pw = [px[i][0:CHUNK] for i in idx]
            x = [x[i] + px[i][CHUNK:] for i in idx]
            yield
        for i, prob in enumerate(probs):
            ares[prob] = dict(lhs2=lhs2[i], a_r=a_r[i], v_bd=v_bd[i], av=av[i],
                              x=(x[i] + _dot(x[i].astype(BF16), bd(pw[i]))).astype(BF16))
        yield

    state = [s_ref[j] for j in range(n_bp)]

    def phase_b(c):
        r0 = c * CHUNK
        bp = [(b, p) for b in range(nb) for p in range(n_pairs)]
        res = [ares[(c, b, p)] for b, p in bp]
        asrs = [_dot_nt(res[j]["lhs2"], bd(state[j])) for j in range(n_bp)]
        yield
        z = [asrs[j][0:CHUNK] + res[j]["av"] for j in range(n_bp)]
        u_p = [_dot(res[j]["x"], bd(z[j])) for j in range(n_bp)]
        yield
        d = [_dot_tn(jnp.concatenate([u_p[j], blk(c, "v", b, p)], axis=0).astype(BF16),
                     jnp.concatenate([blk(c, "bh", b, p), blk(c, "kh", b, p)], axis=0).astype(BF16))
             for j, (b, p) in enumerate(bp)]
        for j, (b, p) in enumerate(bp):
            state[j] = (state[j] * pre[c]["w_last"][b][:, p * LANES:(p + 1) * LANES]
                        + jnp.where(lo_mask, d[j][0:HEAD], d[j][HEAD:]))
        yield
        y = [asrs[j][CHUNK:] + _dot(res[j]["a_r"], jnp.concatenate([bd(u_p[j]), res[j]["v_bd"]], axis=0))
             for j in range(n_bp)]
        y_all = jnp.concatenate(
            [jnp.concatenate([y[b * n_pairs + p] for p in range(n_pairs)], axis=1) for b in range(nb)], axis=0)
        inv_n = 1.0 / HEAD
        yc = y_all - _head_sums(y_all) * inv_n
        var = _head_sums(yc * yc) * inv_n
        yn = yc * lax.rsqrt(var + LNX_EPS) * lng_ref[...] + lnb_ref[...]
        bonus = _head_sums(pre[c]["r"] * pre[c]["k2"] * rk_ref[...]) * pre[c]["v"]
        for b in range(nb):
            bs = slice(b * CHUNK, (b + 1) * CHUNK)
            ya_ref[b, r0:r0 + CHUNK, :] = ((yn[bs] + bonus[bs]) * pre[c]["gate"][bs]).astype(ya_ref.dtype)
        yield

    half_ch = n_ch // 2
    for _ in phase_a(range(half_ch)):
        pass

    def recurrence(chunks):
        for c in chunks:
            yield from phase_b(c)

    b_gen = recurrence(range(half_ch))
    for _ in phase_a(range(half_ch, n_ch)):
        next(b_gen, None)
    for _ in b_gen:
        pass
    hs = []
    for b in range(nb):
        xin = x_ref[b]
        hs.append((xin * lax.rsqrt(jnp.mean(xin * xin, axis=-1, keepdims=True) + RMS_EPS)
                   * ng_ref[...]).astype(BF16))
    n_out = mixw + width
    n_blk = 4 * (n_ch - half_ch)
    edges = [min(n_out, -(-(n_out * i) // (n_blk * LANES)) * LANES) for i in range(n_blk + 1)]
    for k, _ in enumerate(recurrence(range(half_ch, n_ch))):
        for b in range(nb):
            pan_ref[b, :, edges[k]:edges[k + 1]] = _dot(hs[b], win_ref[:, edges[k]:edges[k + 1]])
    for j in range(n_bp):
        s_ref[j] = state[j]
    pa_ref[...] = pan_ref[...]


def _rwkv(x3, width, lora2, params):
    bsz, t, d = x3.shape
    nt = t // ROW_TILE
    a_w = 4 * width + lora2
    full = lambda a: pl.BlockSpec(a.shape, lambda i: (0,) * a.ndim, pipeline_mode=pl.Buffered(1))
    ya = pl.pallas_call(
        functools.partial(_rwkv_kernel, width=width, lora2=lora2),
        grid=(nt,),
        in_specs=[pl.BlockSpec((bsz, ROW_TILE, d), lambda i: (0, 0, 0), pipeline_mode=pl.Buffered(1)),
                  pl.BlockSpec((bsz, ROW_TILE, d), lambda i: (0, jnp.minimum(i + 1, nt - 1), 0))]
        + [full(a) for a in params],
        out_specs=pl.BlockSpec((bsz, ROW_TILE, width), lambda i: (0, i, 0)),
        out_shape=jax.ShapeDtypeStruct((bsz, t, width), BF16),
        scratch_shapes=[
            pltpu.VMEM((bsz, ROW_TILE, a_w), F32),
            pltpu.VMEM((bsz, ROW_TILE, a_w), F32),
            pltpu.VMEM((bsz, 3 * width + lora2), F32),
            pltpu.VMEM((bsz * (width // LANES), HEAD, LANES), F32),
        ],
        compiler_params=pltpu.CompilerParams(
            dimension_semantics=("arbitrary",), vmem_limit_bytes=VMEM_LIMIT),
        name="rwkv",
    )(x3, x3, *params)
    return ya.reshape(bsz * t, width)


def _attn_out_kernel(x_ref, ya_ref, pos_ref, km_ref, vm_ref, ng_ref, win_ref,
                     inv_ref, qg_ref, kg_ref, sink_ref, xqg_ref,
                     wa_ref, wb_ref, wc_ref, wo_ref, o_ref, kprev_ref, vprev_ref,
                     *, sw_w, kv_w, x_w, x_hd):
    tm = x_ref.shape[0]
    d = x_ref.shape[1]
    first = pl.program_id(1) == 0

    @pl.when(first)
    def _():
        kprev_ref[...] = jnp.zeros_like(kprev_ref)
        vprev_ref[...] = jnp.zeros_like(vprev_ref)


    xin = x_ref[...]
    h = (xin * lax.rsqrt(jnp.mean(xin * xin, axis=-1, keepdims=True) + RMS_EPS) * ng_ref[...]).astype(BF16)
    b_w = 2 * sw_w + 2 * kv_w
    c_w = 2 * x_w
    b_0 = win_ref.shape[1] - (b_w + c_w + 3 * d)
    c_0 = b_0 + b_w
    g_0 = c_0 + c_w
    pc = _dot(h, win_ref[:, c_0:g_0])
    n_xh = x_w // x_hd
    xqn = []
    for j in range(n_xh):
        xq = pc[:,j * x_hd:(j + 1) * x_hd]
        xqn.append((xq * lax.rsqrt(jnp.mean(xq * xq, axis=-1, keepdims=True) + RMS_EPS)
                    * (xqg_ref[...] * (x_hd ** -0.5))).astype(BF16))
    s_c = [_dot_nt(xqn[j], km_ref[:, j * x_hd:(j + 1) * x_hd]) for j in range(n_xh)]
    pb = _dot(h, win_ref[:, b_0:c_0])
    pg_a = _dot(h, win_ref[:, g_0:g_0 + d])

    half = HEAD // 2
    n_grp = LANES // half
    qrows = tm // n_grp
    pos = pos_ref[...].astype(F32)
    lane_grp = lax.broadcasted_iota(jnp.int32, (1, LANES), 1) // half
    pos_d = pos[0:qrows]
    for j in range(1, n_grp):
        pos_d = jnp.where(lane_grp == j, pos[j * qrows:(j + 1) * qrows], pos_d)
    ang_d = pos_d * inv_ref[...]
    cos_d = jnp.cos(ang_d)
    sin_d = jnp.sin(ang_d)

    def spread(tab):
        rolled = [tab] + [pltpu.roll(tab, half * k, 1) for k in range(1, n_grp)]
        quarters = []
        for j in range(n_grp):
            cj = rolled[(0 - j) % n_grp]
            for gl in range(1, n_grp):
                cj = jnp.where(lane_grp == gl, rolled[(gl - j) % n_grp], cj)
            quarters.append(cj)
        return jnp.concatenate(quarters, axis=0)

    cos_p = spread(cos_d)
    sin_p = spread(sin_d)

    def rope(xn):
        w = xn.shape[1]
        reps = w // LANES
        first_half = (lax.broadcasted_iota(jnp.int32, (1, w), 1) & (HEAD - 1)) < half
        cos_t = jnp.concatenate([cos_p] * reps, axis=1) if reps > 1 else cos_p
        sin_t = jnp.concatenate([sin_p] * reps, axis=1) if reps > 1 else sin_p
        rot = jnp.where(first_half, -pltpu.roll(xn, w - half, 1), pltpu.roll(xn, half, 1))
        return xn * cos_t + rot * sin_t

    q = pb[:,0:sw_w]
    qn = q * lax.rsqrt(_head_sums(q * q) * (1.0 / HEAD) + RMS_EPS) * (qg_ref[...] * (HEAD ** -0.5))
    qr = rope(qn).astype(BF16)
    kx = pb[:,sw_w:sw_w + kv_w]
    kn = kx * lax.rsqrt(_head_sums(kx * kx) * (1.0 / HEAD) + RMS_EPS) * kg_ref[...]
    kr = rope(kn)
    vx = pb[:,sw_w + kv_w:sw_w + 2 * kv_w]

    lo_mask = lax.broadcasted_iota(jnp.int32, (1, LANES), 1) < HEAD
    qi = lax.broadcasted_iota(jnp.int32, (SW_BLOCK, 2 * SW_BLOCK), 0)
    kj = lax.broadcasted_iota(jnp.int32, (SW_BLOCK, 2 * SW_BLOCK), 1)
    allowed = (kj > qi) & (kj <= qi + SW_BLOCK)
    n_sub = tm // SW_BLOCK
    n_kv = kv_w // HEAD
    n_pairs = sw_w // LANES
    pairs_per_kv = n_pairs // n_kv

    v_bds, masks, s_b = [], [], []
    for sb in range(n_sub):
        rs = slice(sb * SW_BLOCK, (sb + 1) * SW_BLOCK)
        if sb == 0:
            kband = jnp.concatenate([kprev_ref[...], kr[rs]], axis=0)
            vband = jnp.concatenate([vprev_ref[...], vx[rs]], axis=0)
            masks.append(allowed & (kj >= jnp.where(first, SW_BLOCK, 0)))
        else:
            kband = kr[(sb - 1) * SW_BLOCK:(sb + 1) * SW_BLOCK]
            vband = vx[(sb - 1) * SW_BLOCK:(sb + 1) * SW_BLOCK]
            masks.append(allowed)
        kband_sw = pltpu.roll(kband, HEAD, 1)
        vband_sw = pltpu.roll(vband, HEAD, 1)
        for g in range(n_kv):
            k_lo = kband if g == 0 else kband_sw
            k_hi = kband_sw if g == 0 else kband
            v_lo = vband if g == 0 else vband_sw
            v_hi = vband_sw if g == 0 else vband
            k_bd = jnp.concatenate([jnp.where(lo_mask, k_lo, 0.0), jnp.where(lo_mask, 0.0, k_hi)],
                                   axis=0).astype(BF16)
            v_bds.append(jnp.concatenate([jnp.where(lo_mask, v_lo, 0.0), jnp.where(lo_mask, 0.0, v_hi)],
                                         axis=0).astype(BF16))
            for pp in range(pairs_per_kv):
                p = g * pairs_per_kv + pp
                s_b.append(_dot_nt(qr[rs, p * LANES:(p + 1) * LANES], k_bd))
    kprev_ref[...] = kr[tm - SW_BLOCK:, :]
    vprev_ref[...] = vx[tm - SW_BLOCK:, :]

    pg_bc = _dot(h, win_ref[:, g_0 + d:g_0 + 3 * d])
    proj_a = _dot(ya_ref[...], wa_ref[...])

    p_c = []
    for j in range(n_xh):
        m = jnp.max(s_c[j], axis=-1, keepdims=True)
        e = jnp.exp(s_c[j] - m)
        p_c.append((e / jnp.sum(e, axis=-1, keepdims=True)).astype(BF16))
    p_b = []
    for sb in range(n_sub):
        for p in range(n_pairs):
            s2 = s_b[sb * n_pairs + p]
            probs = []
            for hh in range(2):
                s = jnp.where(masks[sb], s2[:, hh * 2 * SW_BLOCK:(hh + 1) * 2 * SW_BLOCK], NEG_INF)
                sink = sink_ref[0, 2 * p + hh]
                m = jnp.maximum(jnp.max(s, axis=-1, keepdims=True), sink)
                e = jnp.exp(s - m)
                den = jnp.sum(e, axis=-1, keepdims=True) + jnp.exp(sink - m)
                probs.append((e / den).astype(BF16))
            p_b.append(jnp.concatenate(probs, axis=1))
    o_c = [_dot(p_c[j], vm_ref[:, j * x_hd:(j + 1) * x_hd]) for j in range(n_xh)]
    o_b = [[_dot(p_b[sb * n_pairs + p], v_bds[sb * n_kv + p // pairs_per_kv]) for p in range(n_pairs)]
           for sb in range(n_sub)]

    y_c = (jnp.concatenate(o_c, axis=1) * _silu(pc[:,x_w:2 * x_w])).astype(BF16)
    proj_c = _dot(y_c, wc_ref[...])
    y_b = jnp.concatenate([jnp.concatenate(o_b[sb], axis=1) for sb in range(n_sub)], axis=0)
    y_b = (y_b * _silu(pb[:,sw_w + 2 * kv_w:2 * sw_w + 2 * kv_w])).astype(BF16)
    proj_b = _dot(y_b, wb_ref[...])
    merged = _sigmoid(pg_a) * proj_a
    merged = merged + _sigmoid(pg_bc[:, d:2 * d]) * proj_c
    merged = merged + _sigmoid(pg_bc[:, 0:d]) * proj_b
    o_ref[...] = x_ref[...] + _dot(merged.astype(BF16), wo_ref[...])


def _attn_out(x2, ya, pos2, km, vm, params, bsz, t, m_len, sw_w, kv_w, x_w, x_hd):
    n, d = x2.shape
    nt = t // ATTN_TILE
    row = lambda a: pl.BlockSpec((ATTN_TILE, a.shape[1]), lambda b, i: (b * nt + i, 0))
    full = lambda a: pl.BlockSpec(a.shape, lambda b, i: (0,) * a.ndim, pipeline_mode=pl.Buffered(1))
    memspec = pl.BlockSpec((m_len, x_w), lambda b, i: (b, 0))
    in_specs = [row(x2), row(ya), row(pos2), memspec, memspec]
    for a in params:
        in_specs.append(full(a))
    in_specs[5 + 5] = pl.BlockSpec(memory_space=pltpu.SMEM)
    return pl.pallas_call(
        functools.partial(_attn_out_kernel, sw_w=sw_w, kv_w=kv_w, x_w=x_w, x_hd=x_hd),
        grid=(bsz, nt),
        in_specs=in_specs,
        out_specs=pl.BlockSpec((ATTN_TILE, d), lambda b, i: (b * nt + i, 0)),
        out_shape=jax.ShapeDtypeStruct((n, d), F32),
        scratch_shapes=[
            pltpu.VMEM((SW_BLOCK, kv_w), F32),
            pltpu.VMEM((SW_BLOCK, kv_w), F32),
        ],
        compiler_params=pltpu.CompilerParams(
            dimension_semantics=("arbitrary", "arbitrary"), vmem_limit_bytes=VMEM_LIMIT),
        name="attn_out",
    )(x2, ya, pos2, km, vm, *params)


def kernel(x, mem, positions, norm_g, mem_norm_g, w_in, mu_rkv, mu_wa, w0, w2, a0, a2, k_k, k_a, r_k,
           lnx_g, lnx_b, q_norm_g, k_norm_g, sinks, xq_norm_g, xk_norm_g, w_mem_kv,
           w_proj_a, w_proj_b, w_proj_c, w_out):
    bsz, t, d = x.shape
    m_len = mem.shape[1]
    depth = w_in.shape[0]
    rw_w = w0.shape[1]
    lora = w2.shape[1]
    sw_w = w_proj_b.shape[1]
    x_w = w_proj_c.shape[1]
    x_hd = xq_norm_g.shape[1]
    kv_w = (w_in.shape[2] - (4 * rw_w + 2 * lora) - 2 * sw_w - 2 * x_w - 3 * d) // 2
    assert t % ROW_TILE == 0 and t % ATTN_TILE == 0 and rw_w % LANES == 0 and sw_w % LANES == 0 and kv_w == LANES
    assert q_norm_g.shape[1] == HEAD and r_k.shape[2] == HEAD and 2 * lora == LANES
    n = bsz * t
    x2 = x.reshape(n, d)
    mem2 = mem.reshape(bsz * m_len, d)
    pos2 = positions.reshape(n, 1)
    half = HEAD // 2
    inv = ROPE_THETA ** (-(jnp.arange(LANES) % half).astype(F32) / half)
    inv = inv.reshape(1, LANES)
    for l in range(depth):
        row = lambda a: a[l].reshape(1, -1)
        w_in_bf = w_in[l].astype(BF16)
        km, vm = _mem_kv(mem2, row(mem_norm_g), w_mem_kv[l].astype(BF16), row(xk_norm_g), m_len, x_w, x_hd)
        mu = jnp.concatenate([mu_rkv[l].reshape(1, -1), mu_wa[l].reshape(1, -1)], axis=1)
        zeros = jnp.zeros((lora, rw_w), F32)
        lora_w = jnp.concatenate([jnp.concatenate([w2[l], zeros], axis=1),
                                  jnp.concatenate([zeros, a2[l]], axis=1)], axis=0)
        lora_hi = lora_w.astype(BF16)
        lora_lo = (lora_w - lora_hi.astype(F32)).astype(BF16)
        t_i = np.arange(bsz * CHUNK)
        tri = jnp.asarray((t_i[:, None] >= t_i[None, :]) & (t_i[:, None] // CHUNK == t_i[None, :] // CHUNK),
                          dtype=BF16)
        rw_params = (row(norm_g), w_in_bf, tri, mu, row(w0), row(a0), lora_hi, lora_lo, row(k_k), row(k_a), row(r_k),
                     row(lnx_g), row(lnx_b))
        ya = _rwkv(x2.reshape(bsz, t, d), rw_w, 2 * lora, rw_params)
        at_params = (row(norm_g), w_in_bf, inv,
                     jnp.tile(row(q_norm_g), (1, sw_w // HEAD)), jnp.tile(row(k_norm_g), (1, kv_w // HEAD)),
                     row(sinks), row(xq_norm_g),
                     w_proj_a[l].astype(BF16), w_proj_b[l].astype(BF16), w_proj_c[l].astype(BF16),
                     w_out[l].astype(BF16))
        x2 = _attn_out(x2, ya, pos2, km, vm, at_params, bsz, t, m_len, sw_w, kv_w, x_w, x_hd)
    return x2.reshape(bsz, t, d)
```

```python
import functools
import math

import jax
import jax.numpy as jnp
import numpy as np
from jax import lax
from jax.experimental import pallas as pl
from jax.experimental.pallas import tpu as pltpu

F32 = jnp.float32
BF16 = jnp.bfloat16

RMS_EPS = 1e-6
LNX_EPS = 64e-5
L2_EPS = 1e-12
DECAY_SCALE = math.exp(-0.5)
ROPE_THETA = 10000.0
NEG_INF = -1e30

HEAD = 64
LANES = 128
CHUNK = 64
SW_BLOCK = 128
ROW_TILE = 256
ATTN_TILE = 512
VMEM_LIMIT = 56 * 1024 * 1024


def _dot(a, b):
    return jnp.dot(a, b, preferred_element_type=F32)


def _dot_nt(a, b):
    return lax.dot_general(a, b, (((1,), (1,)), ((), ())), preferred_element_type=F32)


def _dot_tn(a, b):
    return lax.dot_general(a, b, (((0,), (0,)), ((), ())), preferred_element_type=F32)


def _split2(x):
    hi = x.astype(BF16)
    lo = (x - hi.astype(F32)).astype(BF16)
    return hi, lo


def _split3(x):
    hi = x.astype(BF16)
    r1 = x - hi.astype(F32)
    mid = r1.astype(BF16)
    lo = (r1 - mid.astype(F32)).astype(BF16)
    return hi, mid, lo


def _dot_x3(a, b_hi, b_lo):
    a_hi, a_lo = _split2(a)
    return _dot(a_hi, b_hi) + (_dot(a_hi, b_lo) + _dot(a_lo, b_hi))


def _head_sums(x):
    lo_mask = lax.broadcasted_iota(jnp.int32, (1, LANES), 1) < HEAD
    outs = []
    for p in range(x.shape[1] // LANES):
        xs = x[:, p * LANES:(p + 1) * LANES]
        s_lo = jnp.sum(jnp.where(lo_mask, xs, 0.0), axis=-1, keepdims=True)
        s_hi = jnp.sum(jnp.where(lo_mask, 0.0, xs), axis=-1, keepdims=True)
        outs.append(jnp.where(lo_mask, s_lo, s_hi))
    return outs[0] if len(outs) == 1 else jnp.concatenate(outs, axis=1)


def _sigmoid(x):
    return 0.5 * jnp.tanh(0.5 * x) + 0.5


def _silu(x):
    return x * _sigmoid(x)


def _block_diag2(y, lo_mask):
    return jnp.concatenate([jnp.where(lo_mask, y, 0.0), jnp.where(lo_mask, 0.0, y)], axis=0)


def _mem_kv_kernel(mem_ref, g_ref, w_ref, kg_ref, k_ref, v_ref, *, xw, hd):
    m = mem_ref[...]
    ms = jnp.mean(m * m, axis=-1, keepdims=True)
    h = (m * lax.rsqrt(ms + RMS_EPS) * g_ref[...]).astype(BF16)
    kv = _dot(h, w_ref[...])
    v_ref[...] = kv[:, xw:].astype(v_ref.dtype)
    for j in range(xw // hd):
        kj = kv[:, j * hd:(j + 1) * hd]
        msk = jnp.mean(kj * kj, axis=-1, keepdims=True)
        k_ref[:, j * hd:(j + 1) * hd] = (kj * lax.rsqrt(msk + RMS_EPS) * kg_ref[...]).astype(k_ref.dtype)


def _rwkv_kernel(x0_ref, x_ref, ng_ref, win_ref, tri_ref, mu_ref, w0_ref, a0_ref, lora_hi_ref, lora_lo_ref, kk_ref,
                 ka_ref, rk_ref, lng_ref, lnb_ref, ya_ref, pa_ref, pan_ref, carry_ref, s_ref, *, width, lora2):
    nb, tt, _ = x_ref.shape
    n_pairs = width // LANES
    mixw = 3 * width + lora2
    nrow = nb * CHUNK

    def project(src_ref, dst_ref):
        for b in range(nb):
            xin = src_ref[b]
            h = (xin * lax.rsqrt(jnp.mean(xin * xin, axis=-1, keepdims=True) + RMS_EPS)
                 * ng_ref[...]).astype(BF16)
            dst_ref[b] = _dot(h, win_ref[:, 0:mixw + width])

    @pl.when(pl.program_id(0) == 0)
    def _():
        carry_ref[...] = jnp.zeros_like(carry_ref)
        s_ref[...] = jnp.zeros_like(s_ref)
        project(x0_ref, pa_ref)

    ti = lax.broadcasted_iota(jnp.int32, (CHUNK, LANES), 0)
    li = lax.broadcasted_iota(jnp.int32, (CHUNK, LANES), 1)
    lo_mask = li < HEAD
    lj = jnp.where(lo_mask, li, li - HEAD)
    strict = ti > lj
    incl = ti >= lj
    eye = (ti == lj).astype(F32)
    tri = tri_ref[...]
    row_id = lax.broadcasted_iota(jnp.int32, (nrow, 1), 0)
    mu = mu_ref[...]
    lane_l = lax.broadcasted_iota(jnp.int32, (1, lora2), 1)

    def bd(y):
        return _block_diag2(y, lo_mask).astype(BF16)

    def prep(c):
        r0 = c * CHUNK
        u = jnp.concatenate([pa_ref[b, r0:r0 + CHUNK, 0:mixw] for b in range(nb)], axis=0)
        prev = pltpu.roll(u, 1, 0)
        for b in range(nb):
            before = carry_ref[b:b + 1, :] if c == 0 else pa_ref[b, r0 - 1:r0, 0:mixw]
            prev = jnp.where(row_id == b * CHUNK, before, prev)
        mixed = u + (prev - u) * mu
        r = mixed[:, 0:width]
        k = mixed[:, width:2 * width]
        v = mixed[:, 2 * width:3 * width]
        lo_ra = mixed[:, 3 * width:mixw]
        lo_in = jnp.where(lane_l < lora2 // 2, jnp.tanh(lo_ra), lo_ra)
        proj = _dot_x3(lo_in, lora_hi_ref[...], lora_lo_ref[...])
        lw = -DECAY_SCALE * _sigmoid(w0_ref[...] + proj[:, 0:width])
        a_sig = _sigmoid(a0_ref[...] + proj[:, width:2 * width])
        kk = k * kk_ref[...]
        kkn = kk * lax.rsqrt(jnp.maximum(_head_sums(kk * kk), L2_EPS * L2_EPS))
        k2 = k * (1.0 + (a_sig - 1.0) * ka_ref[...])
        bvec = kkn * a_sig
        l_hi, l_mid, l_lo = _split3(lw)
        cs = _dot(tri, l_hi) + (_dot(tri, l_mid) + _dot(tri, l_lo))
        return dict(r0=r0, r=r, v=v, lw=lw, kkn=kkn, k2=k2, bvec=bvec, cs=cs)

    def prep_tail(q):
        r0, r, v, lw, kkn, k2, bvec, cs = (q[n] for n in ("r0", "r", "v", "lw", "kkn", "k2", "bvec", "cs"))
        cs_last = [cs[(b + 1) * CHUNK - 1:(b + 1) * CHUNK, :] for b in range(nb)]
        cs_end = jnp.concatenate([jnp.broadcast_to(cl, (CHUNK, width)) for cl in cs_last], axis=0)
        w_last = [jnp.exp(cl) for cl in cs_last]
        e_in = jnp.exp(cs)
        e_neg = jnp.exp(-cs)
        e_tail = jnp.exp(cs_end - cs)
        gate = jnp.concatenate([_silu(pa_ref[b, r0:r0 + CHUNK, mixw:mixw + width]) for b in range(nb)], axis=0)
        return dict(r=r, k2=k2, v=v, w_last=w_last, gate=gate, rt=r * e_in, at=-kkn * jnp.exp(cs - lw),
                    kt=k2 * e_neg, bt=bvec * e_neg, kh=k2 * e_tail, bh=bvec * e_tail)

    n_ch = tt // CHUNK
    pre_head = [prep(c) for c in range(n_ch)]

    pre = [prep_tail(q) for q in pre_head]
    for b in range(nb):
        carry_ref[b:b + 1, :] = pa_ref[b, tt - 1:tt, 0:mixw]

    def blk(c, name, b, p):
        return pre[c][name][b * CHUNK:(b + 1) * CHUNK, p * LANES:(p + 1) * LANES]

    n_bp = nb * n_pairs
    ares = {}

    def phase_a(chunks):
        probs = [(c, b, p) for c in chunks for b in range(nb) for p in range(n_pairs)]
        idx = range(len(probs))
        lhs2 = [jnp.concatenate([blk(c, "at", b, p), blk(c, "rt", b, p)], axis=0).astype(BF16)
                for c, b, p in probs]
        g = [_dot_nt(lhs2[i], jnp.concatenate([bd(blk(c, "bt", b, p)), bd(blk(c, "kt", b, p))], axis=0))
             for i, (c, b, p) in enumerate(probs)]
        yield
        a_ab = [jnp.where(strict, g[i][0:CHUNK, 0:LANES], 0.0) for i in idx]
        a_ak = [jnp.where(strict, g[i][0:CHUNK, LANES:], 0.0) for i in idx]
        a_r = [jnp.concatenate([jnp.where(incl, g[i][CHUNK:, 0:LANES], 0.0),
                                jnp.where(incl, g[i][CHUNK:, LANES:], 0.0)], axis=1).astype(BF16) for i in idx]
        v_bd = [bd(blk(c, "v", b, p)) for c, b, p in probs]
        av = [_dot(a_ak[i].astype(BF16), v_bd[i]) for i in idx]
        yield
        x = [eye + a_ab[i] for i in idx]
        pw = [_dot(a_ab[i].astype(BF16), bd(a_ab[i])) for i in idx]
        yield
        for _ in range(4):
            px = [_dot(jnp.concatenate([pw[i], x[i]], axis=0).astype(BF16), bd(pw[i])) for i in idx]
            pw = [px[i][0:CHUNK] for i in idx]
            x = [x[i] + px[i][CHUNK:] for i in idx]
            yield
        for i, prob in enumerate(probs):
            ares[prob] = dict(lhs2=lhs2[i], a_r=a_r[i], v_bd=v_bd[i], av=av[i],
                              x=(x[i] + _dot(x[i].astype(BF16), bd(pw[i]))).astype(BF16))
        yield

    state = [s_ref[j] for j in range(n_bp)]

    def phase_b(c):
        r0 = c * CHUNK
        bp = [(b, p) for b in range(nb) for p in range(n_pairs)]
        res = [ares[(c, b, p)] for b, p in bp]
        asrs = [_dot_nt(res[j]["lhs2"], bd(state[j])) for j in range(n_bp)]
        yield
        z = [asrs[j][0:CHUNK] + res[j]["av"] for j in range(n_bp)]
        u_p = [_dot(res[j]["x"], bd(z[j])) for j in range(n_bp)]
        yield
        d = [_dot_tn(jnp.concatenate([u_p[j], blk(c, "v", b, p)], axis=0).astype(BF16),
                     jnp.concatenate([blk(c, "bh", b, p), blk(c, "kh", b, p)], axis=0).astype(BF16))
             for j, (b, p) in enumerate(bp)]
        for j, (b, p) in enumerate(bp):
            state[j] = (state[j] * pre[c]["w_last"][b][:, p * LANES:(p + 1) * LANES]
                        + jnp.where(lo_mask, d[j][0:HEAD], d[j][HEAD:]))
        yield
        y = [asrs[j][CHUNK:] + _dot(res[j]["a_r"], jnp.concatenate([bd(u_p[j]), res[j]["v_bd"]], axis=0))
             for j in range(n_bp)]
        y_all = jnp.concatenate(
            [jnp.concatenate([y[b * n_pairs + p] for p in range(n_pairs)], axis=1) for b in range(nb)], axis=0)
        inv_n = 1.0 / HEAD
        yc = y_all - _head_sums(y_all) * inv_n
        var = _head_sums(yc * yc) * inv_n
        yn = yc * lax.rsqrt(var + LNX_EPS) * lng_ref[...] + lnb_ref[...]
        bonus = _head_sums(pre[c]["r"] * pre[c]["k2"] * rk_ref[...]) * pre[c]["v"]
        for b in range(nb):
            bs = slice(b * CHUNK, (b + 1) * CHUNK)
            ya_ref[b, r0:r0 + CHUNK, :] = ((yn[bs] + bonus[bs]) * pre[c]["gate"][bs]).astype(ya_ref.dtype)
        yield

    half_ch = n_ch // 2
    for _ in phase_a(range(half_ch)):
        pass

    def recurrence(chunks):
        for c in chunks:
            yield from phase_b(c)

    b_gen = recurrence(range(half_ch))
    for _ in phase_a(range(half_ch, n_ch)):
        next(b_gen, None)
    for _ in b_gen:
        pass
    hs = []
    for b in range(nb):
        xin = x_ref[b]
        hs.append((xin * lax.rsqrt(jnp.mean(xin * xin, axis=-1, keepdims=True) + RMS_EPS)
                   * ng_ref[...]).astype(BF16))
    n_out = mixw + width
    n_blk = 4 * (n_ch - half_ch)
    edges = [min(n_out, -(-(n_out * i) // (n_blk * LANES)) * LANES) for i in range(n_blk + 1)]
    for k, _ in enumerate(recurrence(range(half_ch, n_ch))):
        for b in range(nb):
            pan_ref[b, :, edges[k]:edges[k + 1]] = _dot(hs[b], win_ref[:, edges[k]:edges[k + 1]])
    for j in range(n_bp):
        s_ref[j] = state[j]
    pa_ref[...] = pan_ref[...]


def _rwkv(x3, width, lora2, params):
    bsz, t, d = x3.shape
    nt = t // ROW_TILE
    a_w = 4 * width + lora2
    full = lambda a: pl.BlockSpec(a.shape, lambda i: (0,) * a.ndim, pipeline_mode=pl.Buffered(1))
    ya = pl.pallas_call(
        functools.partial(_rwkv_kernel, width=width, lora2=lora2),
        grid=(nt,),
        in_specs=[pl.BlockSpec((bsz, ROW_TILE, d), lambda i: (0, 0, 0), pipeline_mode=pl.Buffered(1)),
                  pl.BlockSpec((bsz, ROW_TILE, d), lambda i: (0, jnp.minimum(i + 1, nt - 1), 0))]
        + [full(a) for a in params],
        out_specs=pl.BlockSpec((bsz, ROW_TILE, width), lambda i: (0, i, 0)),
        out_shape=jax.ShapeDtypeStruct((bsz, t, width), BF16),
        scratch_shapes=[
            pltpu.VMEM((bsz, ROW_TILE, a_w), F32),
            pltpu.VMEM((bsz, ROW_TILE, a_w), F32),
            pltpu.VMEM((bsz, 3 * width + lora2), F32),
            pltpu.VMEM((bsz * (width // LANES), HEAD, LANES), F32),
        ],
        compiler_params=pltpu.CompilerParams(
            dimension_semantics=("arbitrary",), vmem_limit_bytes=VMEM_LIMIT),
        name="rwkv",
    )(x3, x3, *params)
    return ya.reshape(bsz * t, width)


def _attn_out_kernel(x_ref, ya_ref, pos_ref, mem_ref, mg_ref, wkv_ref, xkg_ref, ng_ref, win_ref,
                     inv_ref, qg_ref, kg_ref, sink_ref, xqg_ref,
                     wa_ref, wb_ref, wc_ref, wo_ref, o_ref, kprev_ref, vprev_ref, km_ref, vm_ref,
                     *, sw_w, kv_w, x_w, x_hd):
    tm = x_ref.shape[0]
    d = x_ref.shape[1]
    first = pl.program_id(1) == 0

    @pl.when(first)
    def _():
        kprev_ref[...] = jnp.zeros_like(kprev_ref)
        vprev_ref[...] = jnp.zeros_like(vprev_ref)
        _mem_kv_kernel(mem_ref, mg_ref, wkv_ref, xkg_ref, km_ref, vm_ref, xw=x_w, hd=x_hd)


    xin = x_ref[...]
    h = (xin * lax.rsqrt(jnp.mean(xin * xin, axis=-1, keepdims=True) + RMS_EPS) * ng_ref[...]).astype(BF16)
    b_w = 2 * sw_w + 2 * kv_w
    c_w = 2 * x_w
    b_0 = win_ref.shape[1] - (b_w + c_w + 3 * d)
    c_0 = b_0 + b_w
    g_0 = c_0 + c_w
    pc = _dot(h, win_ref[:, c_0:g_0])
    n_xh = x_w // x_hd
    xqn = []
    for j in range(n_xh):
        xq = pc[:,j * x_hd:(j + 1) * x_hd]
        xqn.append((xq * lax.rsqrt(jnp.mean(xq * xq, axis=-1, keepdims=True) + RMS_EPS)
                    * (xqg_ref[...] * (x_hd ** -0.5))).astype(BF16))
    s_c = [_dot_nt(xqn[j], km_ref[:, j * x_hd:(j + 1) * x_hd]) for j in range(n_xh)]
    pb = _dot(h, win_ref[:, b_0:c_0])
    pg_a = _dot(h, win_ref[:, g_0:g_0 + d])

    half = HEAD // 2
    n_grp = LANES // half
    qrows = tm // n_grp
    pos = pos_ref[...].astype(F32)
    lane_grp = lax.broadcasted_iota(jnp.int32, (1, LANES), 1) // half
    pos_d = pos[0:qrows]
    for j in range(1, n_grp):
        pos_d = jnp.where(lane_grp == j, pos[j * qrows:(j + 1) * qrows], pos_d)
    ang_d = pos_d * inv_ref[...]
    cos_d = jnp.cos(ang_d)
    sin_d = jnp.sin(ang_d)

    def spread(tab):
        rolled = [tab] + [pltpu.roll(tab, half * k, 1) for k in range(1, n_grp)]
        quarters = []
        for j in range(n_grp):
            cj = rolled[(0 - j) % n_grp]
            for gl in range(1, n_grp):
                cj = jnp.where(lane_grp == gl, rolled[(gl - j) % n_grp], cj)
            quarters.append(cj)
        return jnp.concatenate(quarters, axis=0)

    cos_p = spread(cos_d)
    sin_p = spread(sin_d)

    def rope(xn):
        w = xn.shape[1]
        reps = w // LANES
        first_half = (lax.broadcasted_iota(jnp.int32, (1, w), 1) & (HEAD - 1)) < half
        cos_t = jnp.concatenate([cos_p] * reps, axis=1) if reps > 1 else cos_p
        sin_t = jnp.concatenate([sin_p] * reps, axis=1) if reps > 1 else sin_p
        rot = jnp.where(first_half, -pltpu.roll(xn, w - half, 1), pltpu.roll(xn, half, 1))
        return xn * cos_t + rot * sin_t

    q = pb[:,0:sw_w]
    qn = q * lax.rsqrt(_head_sums(q * q) * (1.0 / HEAD) + RMS_EPS) * (qg_ref[...] * (HEAD ** -0.5))
    qr = rope(qn).astype(BF16)
    kx = pb[:,sw_w:sw_w + kv_w]
    kn = kx * lax.rsqrt(_head_sums(kx * kx) * (1.0 / HEAD) + RMS_EPS) * kg_ref[...]
    kr = rope(kn)
    vx = pb[:,sw_w + kv_w:sw_w + 2 * kv_w]

    lo_mask = lax.broadcasted_iota(jnp.int32, (1, LANES), 1) < HEAD
    qi = lax.broadcasted_iota(jnp.int32, (SW_BLOCK, 2 * SW_BLOCK), 0)
    kj = lax.broadcasted_iota(jnp.int32, (SW_BLOCK, 2 * SW_BLOCK), 1)
    allowed = (kj > qi) & (kj <= qi + SW_BLOCK)
    n_sub = tm // SW_BLOCK
    n_kv = kv_w // HEAD
    n_pairs = sw_w // LANES
    pairs_per_kv = n_pairs // n_kv

    v_bds, masks, s_b = [], [], []
    for sb in range(n_sub):
        rs = slice(sb * SW_BLOCK, (sb + 1) * SW_BLOCK)
        if sb == 0:
            kband = jnp.concatenate([kprev_ref[...], kr[rs]], axis=0)
            vband = jnp.concatenate([vprev_ref[...], vx[rs]], axis=0)
            masks.append(allowed & (kj >= jnp.where(first, SW_BLOCK, 0)))
        else:
            kband = kr[(sb - 1) * SW_BLOCK:(sb + 1) * SW_BLOCK]
            vband = vx[(sb - 1) * SW_BLOCK:(sb + 1) * SW_BLOCK]
            masks.append(allowed)
        kband_sw = pltpu.roll(kband, HEAD, 1)
        vband_sw = pltpu.roll(vband, HEAD, 1)
        for g in range(n_kv):
            k_lo = kband if g == 0 else kband_sw
            k_hi = kband_sw if g == 0 else kband
            v_lo = vband if g == 0 else vband_sw
            v_hi = vband_sw if g == 0 else vband
            k_bd = jnp.concatenate([jnp.where(lo_mask, k_lo, 0.0), jnp.where(lo_mask, 0.0, k_hi)],
                                   axis=0).astype(BF16)
            v_bds.append(jnp.concatenate([jnp.where(lo_mask, v_lo, 0.0), jnp.where(lo_mask, 0.0, v_hi)],
                                         axis=0).astype(BF16))
            for pp in range(pairs_per_kv):
                p = g * pairs_per_kv + pp
                s_b.append(_dot_nt(qr[rs, p * LANES:(p + 1) * LANES], k_bd))
    kprev_ref[...] = kr[tm - SW_BLOCK:, :]
    vprev_ref[...] = vx[tm - SW_BLOCK:, :]

    pg_bc = _dot(h, win_ref[:, g_0 + d:g_0 + 3 * d])
    proj_a = _dot(ya_ref[...], wa_ref[...])

    p_c = []
    for j in range(n_xh):
        m = jnp.max(s_c[j], axis=-1, keepdims=True)
        e = jnp.exp(s_c[j] - m)
        p_c.append((e / jnp.sum(e, axis=-1, keepdims=True)).astype(BF16))
    p_b = []
    for sb in range(n_sub):
        for p in range(n_pairs):
            s2 = s_b[sb * n_pairs + p]
            probs = []
            for hh in range(2):
                s = jnp.where(masks[sb], s2[:, hh * 2 * SW_BLOCK:(hh + 1) * 2 * SW_BLOCK], NEG_INF)
                sink = sink_ref[0, 2 * p + hh]
                m = jnp.maximum(jnp.max(s, axis=-1, keepdims=True), sink)
                e = jnp.exp(s - m)
                den = jnp.sum(e, axis=-1, keepdims=True) + jnp.exp(sink - m)
                probs.append((e / den).astype(BF16))
            p_b.append(jnp.concatenate(probs, axis=1))
    o_c = [_dot(p_c[j], vm_ref[:, j * x_hd:(j + 1) * x_hd]) for j in range(n_xh)]
    o_b = [[_dot(p_b[sb * n_pairs + p], v_bds[sb * n_kv + p // pairs_per_kv]) for p in range(n_pairs)]
           for sb in range(n_sub)]

    y_c = (jnp.concatenate(o_c, axis=1) * _silu(pc[:,x_w:2 * x_w])).astype(BF16)
    proj_c = _dot(y_c, wc_ref[...])
    y_b = jnp.concatenate([jnp.concatenate(o_b[sb], axis=1) for sb in range(n_sub)], axis=0)
    y_b = (y_b * _silu(pb[:,sw_w + 2 * kv_w:2 * sw_w + 2 * kv_w])).astype(BF16)
    proj_b = _dot(y_b, wb_ref[...])
    merged = _sigmoid(pg_a) * proj_a
    merged = merged + _sigmoid(pg_bc[:, d:2 * d]) * proj_c
    merged = merged + _sigmoid(pg_bc[:, 0:d]) * proj_b
    o_ref[...] = x_ref[...] + _dot(merged.astype(BF16), wo_ref[...])


def _attn_out(x2, ya, pos2, mem2, mem_params, params, bsz, t, m_len, sw_w, kv_w, x_w, x_hd):
    n, d = x2.shape
    nt = t // ATTN_TILE
    row = lambda a: pl.BlockSpec((ATTN_TILE, a.shape[1]), lambda b, i: (b * nt + i, 0))
    full = lambda a: pl.BlockSpec(a.shape, lambda b, i: (0,) * a.ndim, pipeline_mode=pl.Buffered(1))
    memspec = pl.BlockSpec((m_len, d), lambda b, i: (b, 0))
    in_specs = [row(x2), row(ya), row(pos2), memspec] + [full(a) for a in mem_params]
    n_lead = len(in_specs)
    for a in params:
        in_specs.append(full(a))
    in_specs[n_lead + 5] = pl.BlockSpec(memory_space=pltpu.SMEM)
    return pl.pallas_call(
        functools.partial(_attn_out_kernel, sw_w=sw_w, kv_w=kv_w, x_w=x_w, x_hd=x_hd),
        grid=(bsz, nt),
        in_specs=in_specs,
        out_specs=pl.BlockSpec((ATTN_TILE, d), lambda b, i: (b * nt + i, 0)),
        out_shape=jax.ShapeDtypeStruct((n, d), F32),
        scratch_shapes=[
            pltpu.VMEM((SW_BLOCK, kv_w), F32),
            pltpu.VMEM((SW_BLOCK, kv_w), F32),
            pltpu.VMEM((m_len, x_w), BF16),
            pltpu.VMEM((m_len, x_w), BF16),
        ],
        compiler_params=pltpu.CompilerParams(
            dimension_semantics=("arbitrary", "arbitrary"), vmem_limit_bytes=VMEM_LIMIT),
        name="attn_out",
    )(x2, ya, pos2, mem2, *mem_params, *params)


def kernel(x, mem, positions, norm_g, mem_norm_g, w_in, mu_rkv, mu_wa, w0, w2, a0, a2, k_k, k_a, r_k,
           lnx_g, lnx_b, q_norm_g, k_norm_g, sinks, xq_norm_g, xk_norm_g, w_mem_kv,
           w_proj_a, w_proj_b, w_proj_c, w_out):
    bsz, t, d = x.shape
    m_len = mem.shape[1]
    depth = w_in.shape[0]
    rw_w = w0.shape[1]
    lora = w2.shape[1]
    sw_w = w_proj_b.shape[1]
    x_w = w_proj_c.shape[1]
    x_hd = xq_norm_g.shape[1]
    kv_w = (w_in.shape[2] - (4 * rw_w + 2 * lora) - 2 * sw_w - 2 * x_w - 3 * d) // 2
    assert t % ROW_TILE == 0 and t % ATTN_TILE == 0 and rw_w % LANES == 0 and sw_w % LANES == 0 and kv_w == LANES
    assert q_norm_g.shape[1] == HEAD and r_k.shape[2] == HEAD and 2 * lora == LANES
    n = bsz * t
    x2 = x.reshape(n, d)
    mem2 = mem.reshape(bsz * m_len, d)
    pos2 = positions.reshape(n, 1)
    half = HEAD // 2
    inv = ROPE_THETA ** (-(jnp.arange(LANES) % half).astype(F32) / half)
    inv = inv.reshape(1, LANES)
    for l in range(depth):
        row = lambda a: a[l].reshape(1, -1)
        w_in_bf = w_in[l].astype(BF16)
        mem_params = (row(mem_norm_g), w_mem_kv[l].astype(BF16), row(xk_norm_g))
        mu = jnp.concatenate([mu_rkv[l].reshape(1, -1), mu_wa[l].reshape(1, -1)], axis=1)
        zeros = jnp.zeros((lora, rw_w), F32)
        lora_w = jnp.concatenate([jnp.concatenate([w2[l], zeros], axis=1),
                                  jnp.concatenate([zeros, a2[l]], axis=1)], axis=0)
        lora_hi = lora_w.astype(BF16)
        lora_lo = (lora_w - lora_hi.astype(F32)).astype(BF16)
        t_i = np.arange(bsz * CHUNK)
        tri = jnp.asarray((t_i[:, None] >= t_i[None, :]) & (t_i[:, None] // CHUNK == t_i[None, :] // CHUNK),
                          dtype=BF16)
        rw_params = (row(norm_g), w_in_bf, tri, mu, row(w0), row(a0), lora_hi, lora_lo, row(k_k), row(k_a), row(r_k),
                     row(lnx_g), row(lnx_b))
        ya = _rwkv(x2.reshape(bsz, t, d), rw_w, 2 * lora, rw_params)
        at_params = (row(norm_g), w_in_bf, inv,
                     jnp.tile(row(q_norm_g), (1, sw_w // HEAD)), jnp.tile(row(k_norm_g), (1, kv_w // HEAD)),
                     row(sinks), row(xq_norm_g),
                     w_proj_a[l].astype(BF16), w_proj_b[l].astype(BF16), w_proj_c[l].astype(BF16),
                     w_out[l].astype(BF16))
        x2 = _attn_out(x2, ya, pos2, mem2, mem_params, at_params, bsz, t, m_len, sw_w, kv_w, x_w, x_hd)
    return x2.reshape(bsz, t, d)
```

```python
import functools
import math

import jax
import jax.numpy as jnp
import numpy as np
from jax import lax
from jax.experimental import pallas as pl
from jax.experimental.pallas import tpu as pltpu

F32 = jnp.float32
BF16 = jnp.bfloat16

RMS_EPS = 1e-6
LNX_EPS = 64e-5
L2_EPS = 1e-12
DECAY_SCALE = math.exp(-0.5)
ROPE_THETA = 10000.0
NEG_INF = -1e30

HEAD = 64
LANES = 128
CHUNK = 64
SW_BLOCK = 128
ROW_TILE = 256
ATTN_TILE = 512
VMEM_LIMIT = 56 * 1024 * 1024


def _dot(a, b):
    return jnp.dot(a, b, preferred_element_type=F32)


def _dot_nt(a, b):
    return lax.dot_general(a, b, (((1,), (1,)), ((), ())), preferred_element_type=F32)


def _dot_tn(a, b):
    return lax.dot_general(a, b, (((0,), (0,)), ((), ())), preferred_element_type=F32)


def _split2(x):
    hi = x.astype(BF16)
    lo = (x - hi.astype(F32)).astype(BF16)
    return hi, lo


def _split3(x):
    hi = x.astype(BF16)
    r1 = x - hi.astype(F32)
    mid = r1.astype(BF16)
    lo = (r1 - mid.astype(F32)).astype(BF16)
    return hi, mid, lo


def _dot_x3(a, b_hi, b_lo):
    a_hi, a_lo = _split2(a)
    return _dot(a_hi, b_hi) + (_dot(a_hi, b_lo) + _dot(a_lo, b_hi))


def _head_sums(x):
    lo_mask = lax.broadcasted_iota(jnp.int32, (1, LANES), 1) < HEAD
    outs = []
    for p in range(x.shape[1] // LANES):
        xs = x[:, p * LANES:(p + 1) * LANES]
        s_lo = jnp.sum(jnp.where(lo_mask, xs, 0.0), axis=-1, keepdims=True)
        s_hi = jnp.sum(jnp.where(lo_mask, 0.0, xs), axis=-1, keepdims=True)
        outs.append(jnp.where(lo_mask, s_lo, s_hi))
    return outs[0] if len(outs) == 1 else jnp.concatenate(outs, axis=1)


def _sigmoid(x):
    return 0.5 * jnp.tanh(0.5 * x) + 0.5


def _silu(x):
    return x * _sigmoid(x)


def _block_diag2(y, lo_mask):
    return jnp.concatenate([jnp.where(lo_mask, y, 0.0), jnp.where(lo_mask, 0.0, y)], axis=0)


def _mem_kv_kernel(mem_ref, g_ref, w_ref, kg_ref, k_ref, v_ref, *, xw, hd):
    m = mem_ref[...]
    ms = jnp.mean(m * m, axis=-1, keepdims=True)
    h = (m * lax.rsqrt(ms + RMS_EPS) * g_ref[...]).astype(BF16)
    kv = _dot(h, w_ref[...])
    v_ref[...] = kv[:, xw:].astype(v_ref.dtype)
    for j in range(xw // hd):
        kj = kv[:, j * hd:(j + 1) * hd]
        msk = jnp.mean(kj * kj, axis=-1, keepdims=True)
        k_ref[:, j * hd:(j + 1) * hd] = (kj * lax.rsqrt(msk + RMS_EPS) * kg_ref[...]).astype(k_ref.dtype)


def _rwkv_kernel(x0_ref, x_ref, ng_ref, win_ref, tri_ref, mu_ref, w0_ref, a0_ref, lora_hi_ref, lora_lo_ref, kk_ref,
                 ka_ref, rk_ref, lng_ref, lnb_ref, ya_ref, pa_ref, pan_ref, carry_ref, s_ref, *, width, lora2):
    nb, tt, _ = x_ref.shape
    n_pairs = width // LANES
    mixw = 3 * width + lora2
    nrow = nb * CHUNK

    def project(src_ref, dst_ref):
        for b in range(nb):
            xin = src_ref[b]
            h = (xin * lax.rsqrt(jnp.mean(xin * xin, axis=-1, keepdims=True) + RMS_EPS)
                 * ng_ref[...]).astype(BF16)
            dst_ref[b] = _dot(h, win_ref[:, 0:mixw + width])

    @pl.when(pl.program_id(0) == 0)
    def _():
        carry_ref[...] = jnp.zeros_like(carry_ref)
        s_ref[...] = jnp.zeros_like(s_ref)
        project(x0_ref, pa_ref)

    ti = lax.broadcasted_iota(jnp.int32, (CHUNK, LANES), 0)
    li = lax.broadcasted_iota(jnp.int32, (CHUNK, LANES), 1)
    lo_mask = li < HEAD
    lj = jnp.where(lo_mask, li, li - HEAD)
    strict = ti > lj
    incl = ti >= lj
    eye = (ti == lj).astype(F32)
    tri = tri_ref[...]
    row_id = lax.broadcasted_iota(jnp.int32, (nrow, 1), 0)
    mu = mu_ref[...]
    lane_l = lax.broadcasted_iota(jnp.int32, (1, lora2), 1)

    def bd(y):
        return _block_diag2(y, lo_mask).astype(BF16)

    def prep(c):
        r0 = c * CHUNK
        u = jnp.concatenate([pa_ref[b, r0:r0 + CHUNK, 0:mixw] for b in range(nb)], axis=0)
        prev = pltpu.roll(u, 1, 0)
        for b in range(nb):
            before = carry_ref[b:b + 1, :] if c == 0 else pa_ref[b, r0 - 1:r0, 0:mixw]
            prev = jnp.where(row_id == b * CHUNK, before, prev)
        mixed = u + (prev - u) * mu
        r = mixed[:, 0:width]
        k = mixed[:, width:2 * width]
        v = mixed[:, 2 * width:3 * width]
        lo_ra = mixed[:, 3 * width:mixw]
        lo_in = jnp.where(lane_l < lora2 // 2, jnp.tanh(lo_ra), lo_ra)
        proj = _dot_x3(lo_in, lora_hi_ref[...], lora_lo_ref[...])
        lw = -DECAY_SCALE * _sigmoid(w0_ref[...] + proj[:, 0:width])
        a_sig = _sigmoid(a0_ref[...] + proj[:, width:2 * width])
        kk = k * kk_ref[...]
        kkn = kk * lax.rsqrt(jnp.maximum(_head_sums(kk * kk), L2_EPS * L2_EPS))
        k2 = k * (1.0 + (a_sig - 1.0) * ka_ref[...])
        bvec = kkn * a_sig
        l_hi, l_mid, l_lo = _split3(lw)
        cs = _dot(tri, l_hi) + (_dot(tri, l_mid) + _dot(tri, l_lo))
        return dict(r0=r0, r=r, v=v, lw=lw, kkn=kkn, k2=k2, bvec=bvec, cs=cs)

    def prep_tail(q):
        r0, r, v, lw, kkn, k2, bvec, cs = (q[n] for n in ("r0", "r", "v", "lw", "kkn", "k2", "bvec", "cs"))
        cs_last = [cs[(b + 1) * CHUNK - 1:(b + 1) * CHUNK, :] for b in range(nb)]
        cs_end = jnp.concatenate([jnp.broadcast_to(cl, (CHUNK, width)) for cl in cs_last], axis=0)
        w_last = [jnp.exp(cl) for cl in cs_last]
        e_in = jnp.exp(cs)
        e_neg = jnp.exp(-cs)
        e_tail = jnp.exp(cs_end - cs)
        gate = jnp.concatenate([_silu(pa_ref[b, r0:r0 + CHUNK, mixw:mixw + width]) for b in range(nb)], axis=0)
        return dict(r=r, k2=k2, v=v, w_last=w_last, gate=gate, rt=r * e_in, at=-kkn * jnp.exp(cs - lw),
                    kt=k2 * e_neg, bt=bvec * e_neg, kh=k2 * e_tail, bh=bvec * e_tail)

    n_ch = tt // CHUNK
    pre_head = [prep(c) for c in range(n_ch)]

    pre = [prep_tail(q) for q in pre_head]
    for b in range(nb):
        carry_ref[b:b + 1, :] = pa_ref[b, tt - 1:tt, 0:mixw]

    def blk(c, name, b, p):
        return pre[c][name][b * CHUNK:(b + 1) * CHUNK, p * LANES:(p + 1) * LANES]

    n_bp = nb * n_pairs
    ares = {}

    def phase_a(chunks):
        probs = [(c, b, p) for c in chunks for b in range(nb) for p in range(n_pairs)]
        idx = range(len(probs))
        lhs2 = [jnp.concatenate([blk(c, "at", b, p), blk(c, "rt", b, p)], axis=0).astype(BF16)
                for c, b, p in probs]
        g = [_dot_nt(lhs2[i], jnp.concatenate([bd(blk(c, "bt", b, p)), bd(blk(c, "kt", b, p))], axis=0))
             for i, (c, b, p) in enumerate(probs)]
        yield
        a_ab = [jnp.where(strict, g[i][0:CHUNK, 0:LANES], 0.0) for i in idx]
        a_ak = [jnp.where(strict, g[i][0:CHUNK, LANES:], 0.0) for i in idx]
        a_r = [jnp.concatenate([jnp.where(incl, g[i][CHUNK:, 0:LANES], 0.0),
                                jnp.where(incl, g[i][CHUNK:, LANES:], 0.0)], axis=1).astype(BF16) for i in idx]
        v_bd = [bd(blk(c, "v", b, p)) for c, b, p in probs]
        av = [_dot(a_ak[i].astype(BF16), v_bd[i]) for i in idx]
        yield
        x = [eye + a_ab[i] for i in idx]
        pw = [_dot(a_ab[i].astype(BF16), bd(a_ab[i])) for i in idx]
        yield
        for _ in range(4):
            px = [_dot(jnp.concatenate([pw[i], x[i]], axis=0).astype(BF16), bd(pw[i])) for i in idx]
            pw = [px[i][0:CHUNK] for i in idx]
            x = [x[i] + px[i][CHUNK:] for i in idx]
            yield
        for i, prob in enumerate(probs):
            ares[prob] = dict(lhs2=lhs2[i], a_r=a_r[i], v_bd=v_bd[i], av=av[i],
                              x=(x[i] + _dot(x[i].astype(BF16), bd(pw[i]))).astype(BF16))
        yield

    state = [s_ref[j] for j in range(n_bp)]

    def phase_b(c):
        r0 = c * CHUNK
        bp = [(b, p) for b in range(nb) for p in range(n_pairs)]
        res = [ares[(c, b, p)] for b, p in bp]
        asrs = [_dot_nt(res[j]["lhs2"], bd(state[j])) for j in range(n_bp)]
        yield
        z = [asrs[j][0:CHUNK] + res[j]["av"] for j in range(n_bp)]
        u_p = [_dot(res[j]["x"], bd(z[j])) for j in range(n_bp)]
        yield
        d = [_dot_tn(jnp.concatenate([u_p[j], blk(c, "v", b, p)], axis=0).astype(BF16),
                     jnp.concatenate([blk(c, "bh", b, p), blk(c, "kh", b, p)], axis=0).astype(BF16))
             for j, (b, p) in enumerate(bp)]
        for j, (b, p) in enumerate(bp):
            state[j] = (state[j] * pre[c]["w_last"][b][:, p * LANES:(p + 1) * LANES]
                        + jnp.where(lo_mask, d[j][0:HEAD], d[j][HEAD:]))
        yield
        y = [asrs[j][CHUNK:] + _dot(res[j]["a_r"], jnp.concatenate([bd(u_p[j]), res[j]["v_bd"]], axis=0))
             for j in range(n_bp)]
        y_all = jnp.concatenate(
            [jnp.concatenate([y[b * n_pairs + p] for p in range(n_pairs)], axis=1) for b in range(nb)], axis=0)
        inv_n = 1.0 / HEAD
        yc = y_all - _head_sums(y_all) * inv_n
        var = _head_sums(yc * yc) * inv_n
        yn = yc * lax.rsqrt(var + LNX_EPS) * lng_ref[...] + lnb_ref[...]
        bonus = _head_sums(pre[c]["r"] * pre[c]["k2"] * rk_ref[...]) * pre[c]["v"]
        for b in range(nb):
            bs = slice(b * CHUNK, (b + 1) * CHUNK)
            ya_ref[b, r0:r0 + CHUNK, :] = ((yn[bs] + bonus[bs]) * pre[c]["gate"][bs]).astype(ya_ref.dtype)
        yield

    half_ch = n_ch // 2
    for _ in phase_a(range(half_ch)):
        pass

    def recurrence(chunks):
        for c in chunks:
            yield from phase_b(c)

    b_gen = recurrence(range(half_ch))
    for _ in phase_a(range(half_ch, n_ch)):
        next(b_gen, None)
    for _ in b_gen:
        pass
    hs = []
    for b in range(nb):
        xin = x_ref[b]
        hs.append((xin * lax.rsqrt(jnp.mean(xin * xin, axis=-1, keepdims=True) + RMS_EPS)
                   * ng_ref[...]).astype(BF16))
    n_out = mixw + width
    n_blk = 4 * (n_ch - half_ch)
    edges = [min(n_out, -(-(n_out * i) // (n_blk * LANES)) * LANES) for i in range(n_blk + 1)]
    for k, _ in enumerate(recurrence(range(half_ch, n_ch))):
        for b in range(nb):
            pan_ref[b, :, edges[k]:edges[k + 1]] = _dot(hs[b], win_ref[:, edges[k]:edges[k + 1]])
    for j in range(n_bp):
        s_ref[j] = state[j]
    pa_ref[...] = pan_ref[...]


def _rwkv(x3, width, lora2, params):
    bsz, t, d = x3.shape
    nt = t // ROW_TILE
    a_w = 4 * width + lora2
    full = lambda a: pl.BlockSpec(a.shape, lambda i: (0,) * a.ndim, pipeline_mode=pl.Buffered(1))
    ya = pl.pallas_call(
        functools.partial(_rwkv_kernel, width=width, lora2=lora2),
        grid=(nt,),
        in_specs=[pl.BlockSpec((bsz, ROW_TILE, d), lambda i: (0, 0, 0), pipeline_mode=pl.Buffered(1)),
                  pl.BlockSpec((bsz, ROW_TILE, d), lambda i: (0, jnp.minimum(i + 1, nt - 1), 0))]
        + [full(a) for a in params],
        out_specs=pl.BlockSpec((bsz, ROW_TILE, width), lambda i: (0, i, 0)),
        out_shape=jax.ShapeDtypeStruct((bsz, t, width), BF16),
        scratch_shapes=[
            pltpu.VMEM((bsz, ROW_TILE, a_w), F32),
            pltpu.VMEM((bsz, ROW_TILE, a_w), F32),
            pltpu.VMEM((bsz, 3 * width + lora2), F32),
            pltpu.VMEM((bsz * (width // LANES), HEAD, LANES), F32),
        ],
        compiler_params=pltpu.CompilerParams(
            dimension_semantics=("arbitrary",), vmem_limit_bytes=VMEM_LIMIT),
        name="rwkv",
    )(x3, x3, *params)
    return ya.reshape(bsz * t, width)


def _attn_out_kernel(x_ref, ya_ref, pos_ref, mem_ref, mg_ref, wkv_ref, xkg_ref, ng_ref, win_ref,
                     inv_ref, qg_ref, kg_ref, sink_ref, xqg_ref,
                     wa_ref, wb_ref, wc_ref, wo_ref, o_ref, kprev_ref, vprev_ref, km_ref, vm_ref,
                     *, sw_w, kv_w, x_w, x_hd):
    tm = x_ref.shape[0]
    d = x_ref.shape[1]
    first = pl.program_id(1) == 0

    @pl.when(first)
    def _():
        kprev_ref[...] = jnp.zeros_like(kprev_ref)
        vprev_ref[...] = jnp.zeros_like(vprev_ref)
        _mem_kv_kernel(mem_ref, mg_ref, wkv_ref, xkg_ref, km_ref, vm_ref, xw=x_w, hd=x_hd)


    xin = x_ref[...]
    h = (xin * lax.rsqrt(jnp.mean(xin * xin, axis=-1, keepdims=True) + RMS_EPS) * ng_ref[...]).astype(BF16)
    b_w = 2 * sw_w + 2 * kv_w
    c_w = 2 * x_w
    b_0 = win_ref.shape[1] - (b_w + c_w + 3 * d)
    c_0 = b_0 + b_w
    g_0 = c_0 + c_w
    pc = _dot(h, win_ref[:, c_0:g_0])
    n_xh = x_w // x_hd
    xqn = []
    for j in range(n_xh):
        xq = pc[:,j * x_hd:(j + 1) * x_hd]
        xqn.append((xq * lax.rsqrt(jnp.mean(xq * xq, axis=-1, keepdims=True) + RMS_EPS)
                    * (xqg_ref[...] * (x_hd ** -0.5))).astype(BF16))
    s_c = [_dot_nt(xqn[j], km_ref[:, j * x_hd:(j + 1) * x_hd]) for j in range(n_xh)]
    pb = _dot(h, win_ref[:, b_0:c_0])
    pg_a = _dot(h, win_ref[:, g_0:g_0 + d])

    half = HEAD // 2
    n_grp = LANES // half
    qrows = tm // n_grp
    lane_grp = lax.broadcasted_iota(jnp.int32, (1, LANES), 1) // half
    pos_d = pos_ref[...].astype(F32)
    ang_d = pos_d * inv_ref[...]
    cos_d = jnp.cos(ang_d)
    sin_d = jnp.sin(ang_d)

    def spread(tab):
        rolled = [tab] + [pltpu.roll(tab, half * k, 1) for k in range(1, n_grp)]
        quarters = []
        for j in range(n_grp):
            cj = rolled[(0 - j) % n_grp]
            for gl in range(1, n_grp):
                cj = jnp.where(lane_grp == gl, rolled[(gl - j) % n_grp], cj)
            quarters.append(cj)
        return jnp.concatenate(quarters, axis=0)

    cos_p = spread(cos_d)
    sin_p = spread(sin_d)

    def rope(xn):
        w = xn.shape[1]
        reps = w // LANES
        first_half = (lax.broadcasted_iota(jnp.int32, (1, w), 1) & (HEAD - 1)) < half
        cos_t = jnp.concatenate([cos_p] * reps, axis=1) if reps > 1 else cos_p
        sin_t = jnp.concatenate([sin_p] * reps, axis=1) if reps > 1 else sin_p
        rot = jnp.where(first_half, -pltpu.roll(xn, w - half, 1), pltpu.roll(xn, half, 1))
        return xn * cos_t + rot * sin_t

    q = pb[:,0:sw_w]
    qn = q * lax.rsqrt(_head_sums(q * q) * (1.0 / HEAD) + RMS_EPS) * (qg_ref[...] * (HEAD ** -0.5))
    qr = rope(qn).astype(BF16)
    kx = pb[:,sw_w:sw_w + kv_w]
    kn = kx * lax.rsqrt(_head_sums(kx * kx) * (1.0 / HEAD) + RMS_EPS) * kg_ref[...]
    kr = rope(kn)
    vx = pb[:,sw_w + kv_w:sw_w + 2 * kv_w]

    lo_mask = lax.broadcasted_iota(jnp.int32, (1, LANES), 1) < HEAD
    qi = lax.broadcasted_iota(jnp.int32, (SW_BLOCK, 2 * SW_BLOCK), 0)
    kj = lax.broadcasted_iota(jnp.int32, (SW_BLOCK, 2 * SW_BLOCK), 1)
    allowed = (kj > qi) & (kj <= qi + SW_BLOCK)
    n_sub = tm // SW_BLOCK
    n_kv = kv_w // HEAD
    n_pairs = sw_w // LANES
    pairs_per_kv = n_pairs // n_kv

    v_bds, masks, s_b = [], [], []
    for sb in range(n_sub):
        rs = slice(sb * SW_BLOCK, (sb + 1) * SW_BLOCK)
        if sb == 0:
            kband = jnp.concatenate([kprev_ref[...], kr[rs]], axis=0)
            vband = jnp.concatenate([vprev_ref[...], vx[rs]], axis=0)
            masks.append(allowed & (kj >= jnp.where(first, SW_BLOCK, 0)))
        else:
            kband = kr[(sb - 1) * SW_BLOCK:(sb + 1) * SW_BLOCK]
            vband = vx[(sb - 1) * SW_BLOCK:(sb + 1) * SW_BLOCK]
            masks.append(allowed)
        kband_sw = pltpu.roll(kband, HEAD, 1)
        vband_sw = pltpu.roll(vband, HEAD, 1)
        for g in range(n_kv):
            k_lo = kband if g == 0 else kband_sw
            k_hi = kband_sw if g == 0 else kband
            v_lo = vband if g == 0 else vband_sw
            v_hi = vband_sw if g == 0 else vband
            k_bd = jnp.concatenate([jnp.where(lo_mask, k_lo, 0.0), jnp.where(lo_mask, 0.0, k_hi)],
                                   axis=0).astype(BF16)
            v_bds.append(jnp.concatenate([jnp.where(lo_mask, v_lo, 0.0), jnp.where(lo_mask, 0.0, v_hi)],
                                         axis=0).astype(BF16))
            for pp in range(pairs_per_kv):
                p = g * pairs_per_kv + pp
                s_b.append(_dot_nt(qr[rs, p * LANES:(p + 1) * LANES], k_bd))
    kprev_ref[...] = kr[tm - SW_BLOCK:, :]
    vprev_ref[...] = vx[tm - SW_BLOCK:, :]

    pg_bc = _dot(h, win_ref[:, g_0 + d:g_0 + 3 * d])
    proj_a = _dot(ya_ref[...], wa_ref[...])

    p_c = []
    for j in range(n_xh):
        m = jnp.max(s_c[j], axis=-1, keepdims=True)
        e = jnp.exp(s_c[j] - m)
        p_c.append((e / jnp.sum(e, axis=-1, keepdims=True)).astype(BF16))
    p_b = []
    for sb in range(n_sub):
        for p in range(n_pairs):
            s2 = s_b[sb * n_pairs + p]
            probs = []
            for hh in range(2):
                s = jnp.where(masks[sb], s2[:, hh * 2 * SW_BLOCK:(hh + 1) * 2 * SW_BLOCK], NEG_INF)
                sink = sink_ref[0, 2 * p + hh]
                m = jnp.maximum(jnp.max(s, axis=-1, keepdims=True), sink)
                e = jnp.exp(s - m)
                den = jnp.sum(e, axis=-1, keepdims=True) + jnp.exp(sink - m)
                probs.append((e / den).astype(BF16))
            p_b.append(jnp.concatenate(probs, axis=1))
    o_c = [_dot(p_c[j], vm_ref[:, j * x_hd:(j + 1) * x_hd]) for j in range(n_xh)]
    o_b = [[_dot(p_b[sb * n_pairs + p], v_bds[sb * n_kv + p // pairs_per_kv]) for p in range(n_pairs)]
           for sb in range(n_sub)]

    y_c = (jnp.concatenate(o_c, axis=1) * _silu(pc[:,x_w:2 * x_w])).astype(BF16)
    proj_c = _dot(y_c, wc_ref[...])
    y_b = jnp.concatenate([jnp.concatenate(o_b[sb], axis=1) for sb in range(n_sub)], axis=0)
    y_b = (y_b * _silu(pb[:,sw_w + 2 * kv_w:2 * sw_w + 2 * kv_w])).astype(BF16)
    proj_b = _dot(y_b, wb_ref[...])
    merged = _sigmoid(pg_a) * proj_a
    merged = merged + _sigmoid(pg_bc[:, d:2 * d]) * proj_c
    merged = merged + _sigmoid(pg_bc[:, 0:d]) * proj_b
    o_ref[...] = x_ref[...] + _dot(merged.astype(BF16), wo_ref[...])


def _attn_out(x2, ya, pos2, mem2, mem_params, params, bsz, t, m_len, sw_w, kv_w, x_w, x_hd):
    n, d = x2.shape
    nt = t // ATTN_TILE
    row = lambda a: pl.BlockSpec((ATTN_TILE, a.shape[1]), lambda b, i: (b * nt + i, 0))
    full = lambda a: pl.BlockSpec(a.shape, lambda b, i: (0,) * a.ndim, pipeline_mode=pl.Buffered(1))
    memspec = pl.BlockSpec((m_len, d), lambda b, i: (b, 0))
    posspec = pl.BlockSpec((pos2.shape[0] // (bsz * nt), LANES), lambda b, i: (b * nt + i, 0))
    in_specs = [row(x2), row(ya), posspec, memspec] + [full(a) for a in mem_params]
    n_lead = len(in_specs)
    for a in params:
        in_specs.append(full(a))
    in_specs[n_lead + 5] = pl.BlockSpec(memory_space=pltpu.SMEM)
    return pl.pallas_call(
        functools.partial(_attn_out_kernel, sw_w=sw_w, kv_w=kv_w, x_w=x_w, x_hd=x_hd),
        grid=(bsz, nt),
        in_specs=in_specs,
        out_specs=pl.BlockSpec((ATTN_TILE, d), lambda b, i: (b * nt + i, 0)),
        out_shape=jax.ShapeDtypeStruct((n, d), F32),
        scratch_shapes=[
            pltpu.VMEM((SW_BLOCK, kv_w), F32),
            pltpu.VMEM((SW_BLOCK, kv_w), F32),
            pltpu.VMEM((m_len, x_w), BF16),
            pltpu.VMEM((m_len, x_w), BF16),
        ],
        compiler_params=pltpu.CompilerParams(
            dimension_semantics=("arbitrary", "arbitrary"), vmem_limit_bytes=VMEM_LIMIT),
        name="attn_out",
    )(x2, ya, pos2, mem2, *mem_params, *params)


def kernel(x, mem, positions, norm_g, mem_norm_g, w_in, mu_rkv, mu_wa, w0, w2, a0, a2, k_k, k_a, r_k,
           lnx_g, lnx_b, q_norm_g, k_norm_g, sinks, xq_norm_g, xk_norm_g, w_mem_kv,
           w_proj_a, w_proj_b, w_proj_c, w_out):
    bsz, t, d = x.shape
    m_len = mem.shape[1]
    depth = w_in.shape[0]
    rw_w = w0.shape[1]
    lora = w2.shape[1]
    sw_w = w_proj_b.shape[1]
    x_w = w_proj_c.shape[1]
    x_hd = xq_norm_g.shape[1]
    kv_w = (w_in.shape[2] - (4 * rw_w + 2 * lora) - 2 * sw_w - 2 * x_w - 3 * d) // 2
    assert t % ROW_TILE == 0 and t % ATTN_TILE == 0 and rw_w % LANES == 0 and sw_w % LANES == 0 and kv_w == LANES
    assert q_norm_g.shape[1] == HEAD and r_k.shape[2] == HEAD and 2 * lora == LANES
    n = bsz * t
    x2 = x.reshape(n, d)
    mem2 = mem.reshape(bsz * m_len, d)
    n_grp = 2 * LANES // HEAD
    pos2 = jnp.repeat(positions.reshape(n // ATTN_TILE, n_grp, ATTN_TILE // n_grp).transpose(0, 2, 1),
                      HEAD // 2, axis=2).reshape(n // n_grp, LANES)
    half = HEAD // 2
    inv = ROPE_THETA ** (-(jnp.arange(LANES) % half).astype(F32) / half)
    inv = inv.reshape(1, LANES)
    for l in range(depth):
        row = lambda a: a[l].reshape(1, -1)
        w_in_bf = w_in[l].astype(BF16)
        mem_params = (row(mem_norm_g), w_mem_kv[l].astype(BF16), row(xk_norm_g))
        mu = jnp.concatenate([mu_rkv[l].reshape(1, -1), mu_wa[l].reshape(1, -1)], axis=1)
        zeros = jnp.zeros((lora, rw_w), F32)
        lora_w = jnp.concatenate([jnp.concatenate([w2[l], zeros], axis=1),
                                  jnp.concatenate([zeros, a2[l]], axis=1)], axis=0)
        lora_hi = lora_w.astype(BF16)
        lora_lo = (lora_w - lora_hi.astype(F32)).astype(BF16)
        t_i = np.arange(bsz * CHUNK)
        tri = jnp.asarray((t_i[:, None] >= t_i[None, :]) & (t_i[:, None] // CHUNK == t_i[None, :] // CHUNK),
                          dtype=BF16)
        rw_params = (row(norm_g), w_in_bf, tri, mu, row(w0), row(a0), lora_hi, lora_lo, row(k_k), row(k_a), row(r_k),
                     row(lnx_g), row(lnx_b))
        ya = _rwkv(x2.reshape(bsz, t, d), rw_w, 2 * lora, rw_params)
        at_params = (row(norm_g), w_in_bf, inv,
                     jnp.tile(row(q_norm_g), (1, sw_w // HEAD)), jnp.tile(row(k_norm_g), (1, kv_w // HEAD)),
                     row(sinks), row(xq_norm_g),
                     w_proj_a[l].astype(BF16), w_proj_b[l].astype(BF16), w_proj_c[l].astype(BF16),
                     w_out[l].astype(BF16))
        x2 = _attn_out(x2, ya, pos2, mem2, mem_params, at_params, bsz, t, m_len, sw_w, kv_w, x_w, x_hd)
    return x2.reshape(bsz, t, d)
```
